```python
import jax, jax.numpy as jnp
from jax import lax
import numpy as np

D_MODEL = 1024
BATCH = 8
SEQ = 4096
DEPTH = 2

N_META = 16
D_FF = 2816
EPS = 1e-6
POOL_WINDOWS = (2, 4, 8, 16)
POOL_GROUP = D_MODEL // 16
D_POOL = POOL_GROUP * len(POOL_WINDOWS)
HG_HEAD_K = 128
HG_HEAD_V = 128
D_HGRN = D_MODEL - D_POOL
HG_HEADS = D_HGRN // HG_HEAD_K
HG_CHUNK = 64
D_IN_EVEN = D_POOL + 4 * D_HGRN
D_CONV = D_MODEL // 2
CONV_WIDTH = 31
CONV_GROUPS = 4
D_LRU = D_MODEL // 2
LRU_HEADS = 8
LRU_HEAD = D_LRU // LRU_HEADS
LRU_CONV = 4
LRU_C = 8.0
D_IN_ODD = 2 * D_CONV + 2 * D_LRU
N_EVEN = (DEPTH + 1) // 2
N_ODD = DEPTH // 2

kernel_name = "hybrid_pool_hgrn2_conv_rglru_macaron"


def rms_norm(x, g):
    xf = x.astype(jnp.float32)
    y = xf * lax.rsqrt(jnp.mean(xf * xf, axis=-1, keepdims=True) + EPS)
    return (y * g.astype(jnp.float32)).astype(x.dtype)


def swiglu_ffn(x, wg, wu, wd):
    return (jax.nn.silu(x @ wg) * (x @ wu)) @ wd


def causal_depthwise_conv(x, w, b):
    width = w.shape[0]
    xp = jnp.pad(x, ((0, 0), (width - 1, 0), (0, 0)))
    y = lax.conv_general_dilated(xp, w.astype(x.dtype)[:, None, :], window_strides=(1,), padding='VALID',
                                 dimension_numbers=('NWC', 'WIO', 'NWC'), feature_group_count=x.shape[-1])
    return y + b.astype(x.dtype)


def multiscale_pool(u, w_grp, scale):
    bn, L, _ = u.shape
    ug = u.astype(jnp.float32).reshape(bn, L, len(POOL_WINDOWS), POOL_GROUP)
    c = jnp.cumsum(ug, axis=1)
    t = jnp.arange(1, L + 1, dtype=jnp.float32)[None, :, None]
    pooled = []
    for gi, w in enumerate(POOL_WINDOWS):
        cg = c[:, :, gi]
        lagged = jnp.pad(cg, ((0, 0), (w, 0), (0, 0)))[:, :L]
        pooled.append((cg - lagged) / jnp.minimum(t, float(w)))
    mixed = jnp.stack(pooled, axis=2) - ug
    y = jnp.einsum('blgc,gcd->blgd', mixed, w_grp.astype(jnp.float32))
    return (y.reshape(bn, L, D_POOL) * scale.astype(jnp.float32)).astype(u.dtype)


def hgrn2_mixer(q_raw, f_raw, i_raw, g_raw, lb, gnorm):
    f32 = jnp.float32
    bn, L, _ = q_raw.shape
    q = jax.nn.silu(q_raw.astype(f32))
    z = f_raw.astype(f32)
    lbf = lb.astype(f32)
    log_f = jnp.logaddexp(jnp.log(lbf), jnp.log1p(-lbf) + jax.nn.log_sigmoid(z))
    k = (1.0 - lbf) * jax.nn.sigmoid(-z)
    v = i_raw.astype(f32)
    pad = (-N_META) % HG_CHUNK
    n_chunks = (L + pad) // HG_CHUNK

    def to_chunks(a):
        a = jnp.pad(a, ((0, 0), (pad, 0), (0, 0)))
        return a.reshape(bn, n_chunks, HG_CHUNK, HG_HEADS, -1).transpose(1, 0, 3, 2, 4)

    causal = jnp.tril(jnp.ones((HG_CHUNK, HG_CHUNK), dtype=bool))[:, :, None]

    def step(S, inp):
        qc, kc, lfc, vc = inp
        b = jnp.cumsum(lfc, axis=2)
        o_inter = jnp.einsum('bhtk,bhkv->bhtv', qc * jnp.exp(b), S)
        diff = b[:, :, :, None, :] - b[:, :, None, :, :]
        decay = jnp.exp(jnp.where(causal, diff, -jnp.inf))
        A = jnp.einsum('bhtsk,bhsk->bhts', qc[:, :, :, None, :] * decay, kc)
        o = o_inter + jnp.einsum('bhts,bhsv->bhtv', A, vc)
        b_last = b[:, :, -1:, :]
        S_new = jnp.exp(b_last[:, :, 0, :])[..., None] * S + jnp.einsum('bhsk,bhsv->bhkv', kc * jnp.exp(b_last - b), vc)
        return S_new, o

    S0 = jnp.zeros((bn, HG_HEADS, HG_HEAD_K, HG_HEAD_V), f32)
    _, o = lax.scan(step, S0, (to_chunks(q), to_chunks(k), to_chunks(log_f), to_chunks(v)))
    o = o.transpose(1, 0, 3, 2, 4).reshape(bn, n_chunks * HG_CHUNK, HG_HEADS, HG_HEAD_V)[:, pad:]
    o = o * lax.rsqrt(jnp.mean(o * o, axis=-1, keepdims=True) + EPS) * gnorm.astype(f32)
    o = o * jax.nn.silu(g_raw.astype(f32)).reshape(bn, L, HG_HEADS, HG_HEAD_V)
    return o.reshape(bn, L, D_HGRN).astype(q_raw.dtype)


def conformer_conv_module(a, b, w, bias, ln_g, ln_b):
    f32 = jnp.float32
    bn, L, _ = a.shape
    u = a * jax.nn.sigmoid(b)
    u = causal_depthwise_conv(u, w, bias).astype(f32).reshape(bn, L, CONV_GROUPS, D_CONV // CONV_GROUPS)
    mu = jnp.mean(u, axis=-1, keepdims=True)
    var = jnp.mean(jnp.square(u - mu), axis=-1, keepdims=True)
    un = ((u - mu) * lax.rsqrt(var + EPS)).reshape(bn, L, D_CONV) * ln_g.astype(f32) + ln_b.astype(f32)
    return jax.nn.silu(un).astype(a.dtype)


def rglru_block(xb, gate, conv_w, conv_b, wa, ba, wx, bx, lam):
    f32 = jnp.float32
    u = causal_depthwise_conv(xb, conv_w, conv_b).astype(f32)
    bn, L, _ = u.shape
    uh = u.reshape(bn, L, LRU_HEADS, LRU_HEAD)
    r = jax.nn.sigmoid(jnp.einsum('blhi,hij->blhj', uh, wa.astype(f32)).reshape(bn, L, D_LRU) + ba.astype(f32))
    i = jax.nn.sigmoid(jnp.einsum('blhi,hij->blhj', uh, wx.astype(f32)).reshape(bn, L, D_LRU) + bx.astype(f32))
    log_a = -LRU_C * r * jax.nn.softplus(-lam.astype(f32))
    a = jnp.exp(log_a)
    mult = jnp.sqrt(-jnp.expm1(2.0 * log_a))
    reset = (jnp.arange(L) == 0)[None, :, None]
    bterm = jnp.where(reset, 1.0, mult) * (i * u)

    def combine(c1, c2):
        a1, b1 = c1
        a2, b2 = c2
        return a1 * a2, a2 * b1 + b2

    _, h = lax.associative_scan(combine, (a, bterm), axis=1)
    return (jax.nn.gelu(gate.astype(f32)) * h).astype(xb.dtype)


def setup_inputs(seed: int = 0) -> dict:
    key = jax.random.key(seed)
    ks = iter(jax.random.split(key, 48))
    f32 = jnp.float32

    def nrm(shape, scale):
        return jax.random.normal(next(ks), shape, f32) * scale

    def gain(shape):
        return 1.0 + 0.05 * jax.random.normal(next(ks), shape, f32)

    a0 = jax.random.uniform(next(ks), (N_ODD, D_LRU), f32, 0.9, 0.999)
    s = a0 ** (1.0 / LRU_C)
    lam = jnp.log(s) - jnp.log1p(-s)
    return {
        "x": nrm((BATCH, SEQ, D_MODEL), 1.0),
        "meta_tokens": nrm((N_META, D_MODEL), 1.0),
        "ffn1_norm": gain((DEPTH, D_MODEL)),
        "ffn1_wg": nrm((DEPTH, D_MODEL, D_FF), D_MODEL ** -0.5),
        "ffn1_wu": nrm((DEPTH, D_MODEL, D_FF), D_MODEL ** -0.5),
        "ffn1_wd": nrm((DEPTH, D_FF, D_MODEL), D_FF ** -0.5),
        "mix_norm": gain((DEPTH, D_MODEL)),
        "ffn2_norm": gain((DEPTH, D_MODEL)),
        "ffn2_wg": nrm((DEPTH, D_MODEL, D_FF), D_MODEL ** -0.5),
        "ffn2_wu": nrm((DEPTH, D_MODEL, D_FF), D_MODEL ** -0.5),
        "ffn2_wd": nrm((DEPTH, D_FF, D_MODEL), D_FF ** -0.5),
        "w_in_even": nrm((N_EVEN, D_MODEL, D_IN_EVEN), D_MODEL ** -0.5),
        "pool_w": nrm((N_EVEN, len(POOL_WINDOWS), POOL_GROUP, POOL_GROUP), POOL_GROUP ** -0.5),
        "pool_scale": gain((N_EVEN, D_POOL)),
        "hgrn_lb_logits": nrm((N_EVEN + 1, D_HGRN), 0.5),
        "hgrn_gnorm": gain((N_EVEN, HG_HEAD_V)),
        "w_out_even": nrm((N_EVEN, D_POOL + D_HGRN, D_MODEL), (D_POOL + D_HGRN) ** -0.5),
        "w_in_odd": nrm((N_ODD, D_MODEL, D_IN_ODD), D_MODEL ** -0.5),
        "conv_w": nrm((N_ODD, CONV_WIDTH, D_CONV), CONV_WIDTH ** -0.5),
        "conv_b": nrm((N_ODD, D_CONV), 0.02),
        "conv_ln_g": gain((N_ODD, D_CONV)),
        "conv_ln_b": nrm((N_ODD, D_CONV), 0.02),
        "lru_conv_w": nrm((N_ODD, LRU_CONV, D_LRU), LRU_CONV ** -0.5),
        "lru_conv_b": nrm((N_ODD, D_LRU), 0.02),
        "lru_wa": nrm((N_ODD, LRU_HEADS, LRU_HEAD, LRU_HEAD), LRU_HEAD ** -0.5),
        "lru_ba": nrm((N_ODD, D_LRU), 0.02),
        "lru_wx": nrm((N_ODD, LRU_HEADS, LRU_HEAD, LRU_HEAD), LRU_HEAD ** -0.5),
        "lru_bx": nrm((N_ODD, D_LRU), 0.02),
        "lru_lambda": lam,
        "w_out_odd": nrm((N_ODD, D_CONV + D_LRU, D_MODEL), (D_CONV + D_LRU) ** -0.5),
        "final_norm": gain((D_MODEL,)),
    }


def reference(x, meta_tokens, ffn1_norm, ffn1_wg, ffn1_wu, ffn1_wd, mix_norm, ffn2_norm, ffn2_wg, ffn2_wu,
              ffn2_wd, w_in_even, pool_w, pool_scale, hgrn_lb_logits, hgrn_gnorm, w_out_even, w_in_odd,
              conv_w, conv_b, conv_ln_g, conv_ln_b, lru_conv_w, lru_conv_b, lru_wa, lru_ba, lru_wx, lru_bx,
              lru_lambda, w_out_odd, final_norm):
    bn = x.shape[0]
    meta = jnp.broadcast_to(meta_tokens.astype(x.dtype)[None], (bn, N_META, D_MODEL))
    h = jnp.concatenate([meta, x], axis=1)
    lbs = jnp.cumsum(jax.nn.softmax(hgrn_lb_logits.astype(jnp.float32), axis=0), axis=0)
    for l in range(DEPTH):
        j = l // 2
        h = h + 0.5 * swiglu_ffn(rms_norm(h, ffn1_norm[l]), ffn1_wg[l], ffn1_wu[l], ffn1_wd[l])
        u = rms_norm(h, mix_norm[l])
        if l % 2 == 0:
            p = u @ w_in_even[j]
            p_pool, q_r, f_r, i_r, g_r = jnp.split(
                p, [D_POOL, D_POOL + D_HGRN, D_POOL + 2 * D_HGRN, D_POOL + 3 * D_HGRN], axis=-1)
            ya = multiscale_pool(p_pool, pool_w[j], pool_scale[j])
            yb = hgrn2_mixer(q_r, f_r, i_r, g_r, lbs[j], hgrn_gnorm[j])
            y = jnp.concatenate([ya, yb], axis=-1) @ w_out_even[j]
        else:
            p = u @ w_in_odd[j]
            c_a, c_b, d_x, d_g = jnp.split(p, [D_CONV, 2 * D_CONV, 2 * D_CONV + D_LRU], axis=-1)
            yc = conformer_conv_module(c_a, c_b, conv_w[j], conv_b[j], conv_ln_g[j], conv_ln_b[j])
            yd = rglru_block(d_x, d_g, lru_conv_w[j], lru_conv_b[j], lru_wa[j], lru_ba[j], lru_wx[j],
                             lru_bx[j], lru_lambda[j])
            y = jnp.concatenate([yc, yd], axis=-1) @ w_out_odd[j]
        h = h + y
        h = h + 0.5 * swiglu_ffn(rms_norm(h, ffn2_norm[l]), ffn2_wg[l], ffn2_wu[l], ffn2_wd[l])
    h = rms_norm(h, final_norm)
    return h[:, N_META:]
```

```python
import functools

import jax
import jax.numpy as jnp
from jax import lax
from jax.experimental import pallas as pl
from jax.experimental.pallas import tpu as pltpu

F32 = jnp.float32
BF16 = jnp.bfloat16

D_MODEL = 1024
BATCH = 8
SEQ = 4096
N_META = 16
D_FF = 2816
EPS = 1e-6
POOL_WINDOWS = (2, 4, 8, 16)
POOL_GROUP = 64
D_POOL = 256
HG_HEAD = 128
D_HGRN = 768
HG_HEADS = 6
CHUNK = 64
D_IN_EVEN = D_POOL + 4 * D_HGRN
D_CONV = 512
CONV_WIDTH = 31
CONV_GROUPS = 4
D_LRU = 512
LRU_HEADS = 8
LRU_HEAD = 64
LRU_CONV = 4
LRU_C = 8.0
D_IN_ODD = 2 * D_CONV + 2 * D_LRU

L_REAL = N_META + SEQ
L_PAD = ((L_REAL + CHUNK - 1) // CHUNK) * CHUNK
ROWS = BATCH * L_PAD

TM_FFN = 320
FF_CHUNK = 1408
TL_MIX = 320
HIST = 32
VMEM_LIMIT = 56 * 1024 * 1024


def _rms(x, g):
    ms = jnp.mean(x * x, axis=-1, keepdims=True)
    return x * lax.rsqrt(ms + EPS) * g


def _sigmoid(x):
    return 1.0 / (1.0 + jnp.exp(-x))


def _silu(x):
    return x * _sigmoid(x)


def _dot(a, b):
    return jnp.dot(a, b, preferred_element_type=F32)


def _dot_nt(a, b):
    return lax.dot_general(a, b, (((1,), (1,)), ((), ())), preferred_element_type=F32)


def _dot_tn(a, b):
    return lax.dot_general(a, b, (((0,), (0,)), ((), ())), preferred_element_type=F32)


def _swiglu(xn, wg_ref, wu_ref, wd_ref):
    acc = None
    for c in range(D_FF // FF_CHUNK):
        sl = slice(c * FF_CHUNK, (c + 1) * FF_CHUNK)
        gate = _dot(xn, wg_ref[:, sl])
        up = _dot(xn, wu_ref[:, sl])
        hid = (_silu(gate) * up).astype(BF16)
        part = _dot(hid, wd_ref[sl, :])
        acc = part if acc is None else acc + part
    return acc


def _ffn_kernel(n_ffn, final, *refs):
    h_ref = refs[0]
    out_ref = refs[-1]
    h = h_ref[...]
    for i in range(n_ffn):
        g_ref, wg_ref, wu_ref, wd_ref = refs[1 + 4 * i: 5 + 4 * i]
        xn = _rms(h, g_ref[...]).astype(BF16)
        h = h + 0.5 * _swiglu(xn, wg_ref, wu_ref, wd_ref)
    if final:
        h = _rms(h, refs[1 + 4 * n_ffn][...])
    out_ref[...] = h


def _resident():
    return pl.BlockSpec(memory_space=pltpu.VMEM)


def _ffn_call(h, ffns, final_g=None):
    args = [h]
    in_specs = [pl.BlockSpec((TM_FFN, D_MODEL), lambda i: (i, 0))]
    for g, wg, wu, wd in ffns:
        args += [g, wg, wu, wd]
        in_specs += [_resident()] * 4
    if final_g is not None:
        args.append(final_g)
        in_specs.append(_resident())
    return pl.pallas_call(
        functools.partial(_ffn_kernel, len(ffns), final_g is not None),
        out_shape=jax.ShapeDtypeStruct((ROWS, D_MODEL), F32),
        grid=(ROWS // TM_FFN,),
        in_specs=in_specs,
        out_specs=pl.BlockSpec((TM_FFN, D_MODEL), lambda i: (i, 0)),
        compiler_params=pltpu.CompilerParams(
            dimension_semantics=("parallel",), vmem_limit_bytes=VMEM_LIMIT),
    )(*args)


def _even_kernel(h_ref, g_ref, win_ref, poolw_ref, pscale_ref, lbl_ref, gn_ref, wout_ref, tril_ref,
                 out_ref, st_ref, xh_ref, p_ref, y_ref):
    t = pl.program_id(1)
    TL = TL_MIX

    @pl.when(t == 0)
    def _():
        st_ref[...] = jnp.zeros_like(st_ref)
        xh_ref[0:HIST, :] = jnp.zeros((HIST, D_POOL), F32)

    h = h_ref[...]
    u = _rms(h, g_ref[...]).astype(BF16)
    p_ref[...] = _dot(u, win_ref[...])

    x = p_ref[:, 0:D_POOL]
    xh_ref[HIST:HIST + TL, :] = x
    lag = lambda j: xh_ref[HIST - j:HIST - j + TL, :]
    s2 = x + lag(1)
    s4 = s2 + lag(2) + lag(3)
    s8 = s4 + lag(4) + lag(5) + lag(6) + lag(7)
    s16 = s8
    for j in range(8, 16):
        s16 = s16 + lag(j)
    lane = lax.broadcasted_iota(jnp.int32, (TL, D_POOL), 1)
    pos = (lax.broadcasted_iota(jnp.int32, (TL, D_POOL), 0) + t * TL + 1).astype(F32)
    win = jnp.where(lane < 64, 2.0, jnp.where(lane < 128, 4.0, jnp.where(lane < 192, 8.0, 16.0)))
    ssel = jnp.where(lane < 64, s2, jnp.where(lane < 128, s4, jnp.where(lane < 192, s8, s16)))
    mixed = ssel / jnp.minimum(pos, win) - x
    ya = _dot(mixed.astype(BF16), poolw_ref[...]) * pscale_ref[...]
    y_ref[:, 0:D_POOL] = ya.astype(BF16)
    xh_ref[0:HIST, :] = xh_ref[TL:TL + HIST, :]

    lbl = lbl_ref[...]
    mx = jnp.maximum(lbl[0:1, :], lbl[1:2, :])
    e0 = jnp.exp(lbl[0:1, :] - mx)
    e1 = jnp.exp(lbl[1:2, :] - mx)
    lb = e0 / (e0 + e1)
    log_lb = jnp.log(lb)
    log_1mlb = jnp.log1p(-lb)
    one_mlb = 1.0 - lb
    gn = gn_ref[...]
    tril = tril_ref[...]
    causal = tril > 0.5

    def chunk_body(c, carry):
        r0 = pl.multiple_of(c * CHUNK, CHUNK)
        rows = pl.ds(r0, CHUNK)
        q = _silu(p_ref[rows, D_POOL:D_POOL + D_HGRN])
        z = p_ref[rows, D_POOL + D_HGRN:D_POOL + 2 * D_HGRN]
        v = p_ref[rows, D_POOL + 2 * D_HGRN:D_POOL + 3 * D_HGRN].astype(BF16)
        gg = p_ref[rows, D_POOL + 3 * D_HGRN:D_POOL + 4 * D_HGRN]
        ez = jnp.exp(-jnp.abs(z))
        rz = 1.0 / (1.0 + ez)
        log_sig = jnp.minimum(z, 0.0) - jnp.log1p(ez)
        sig_neg = jnp.where(z >= 0, ez * rz, rz)
        bb = log_1mlb + log_sig
        log_f = jnp.maximum(log_lb, bb) + jnp.log1p(jnp.exp(-jnp.abs(log_lb - bb)))
        k = one_mlb * sig_neg
        b = jnp.dot(tril, log_f, precision=lax.Precision.HIGHEST, preferred_element_type=F32)
        b_end = b[CHUNK - 1:CHUNK, :]
        mid = 0.5 * b_end
        e_mid = jnp.exp(mid)
        qt = q * jnp.exp(b - mid)
        kt = k * jnp.exp(mid - b)
        qi = (qt * e_mid).astype(BF16)
        ks = (kt * e_mid).astype(BF16)
        qt = qt.astype(BF16)
        kt = kt.astype(BF16)
        dec = jnp.exp(b_end)
        for hd in range(HG_HEADS):
            hs = slice(hd * HG_HEAD, (hd + 1) * HG_HEAD)
            a = jnp.where(causal, _dot_nt(qt[:, hs], kt[:, hs]), 0.0)
            st = st_ref[hd]
            o = _dot(a.astype(BF16), v[:, hs]) + _dot_nt(qi[:, hs], st.astype(BF16))
            st_ref[hd] = st * dec[:, hs] + _dot_tn(v[:, hs], ks[:, hs])
            o = o * lax.rsqrt(jnp.mean(o * o, axis=-1, keepdims=True) + EPS) * gn
            o = o * _silu(gg[:, hs])
            y_ref[rows, D_POOL + hd * HG_HEAD:D_POOL + (hd + 1) * HG_HEAD] = o.astype(BF16)
        return carry

    lax.fori_loop(0, TL // CHUNK, chunk_body, 0)
    out_ref[...] = h + _dot(y_ref[...], wout_ref[...])


def _even_call(h3, g, win, poolw, pscale, lbl, gn, wout, tril):
    row_spec = pl.BlockSpec((None, TL_MIX, D_MODEL), lambda b, t: (b, t, 0))
    return pl.pallas_call(
        _even_kernel,
        out_shape=jax.ShapeDtypeStruct((BATCH, L_PAD, D_MODEL), F32),
        grid=(BATCH, L_PAD // TL_MIX),
        in_specs=[row_spec] + [_resident()] * 8,
        out_specs=row_spec,
        scratch_shapes=[
            pltpu.VMEM((HG_HEADS, HG_HEAD, HG_HEAD), F32),
            pltpu.VMEM((HIST + TL_MIX, D_POOL), F32),
            pltpu.VMEM((TL_MIX, D_IN_EVEN), F32),
            pltpu.VMEM((TL_MIX, D_MODEL), BF16),
        ],
        compiler_params=pltpu.CompilerParams(
            dimension_semantics=("parallel", "arbitrary"), vmem_limit_bytes=VMEM_LIMIT),
    )(h3, g, win, poolw, pscale, lbl, gn, wout, tril)


CONV_RB = 64


def _odd_kernel(h_ref, g_ref, win_ref, cw_ref, cb_ref, lng_ref, lnb_ref, lw_ref, lcb_ref, wax_ref, bax_ref,
                lam_ref, wout_ref, out_ref, uh_ref, xh_ref, hc_ref, p_ref, y_ref, sa_ref, sb_ref):
    t = pl.program_id(1)
    TL = TL_MIX

    @pl.when(t == 0)
    def _():
        uh_ref[0:HIST, :] = jnp.zeros((HIST, D_CONV), F32)
        xh_ref[0:HIST, :] = jnp.zeros((HIST, D_LRU), F32)
        hc_ref[...] = jnp.zeros_like(hc_ref)

    h = h_ref[...]
    un = _rms(h, g_ref[...]).astype(BF16)
    p_ref[...] = _dot(un, win_ref[...])

    uh_ref[HIST:HIST + TL, :] = p_ref[:, 0:D_CONV] * _sigmoid(p_ref[:, D_CONV:2 * D_CONV])
    base = HIST - (CONV_WIDTH - 1)
    for rb in range(TL // CONV_RB):
        acc = jnp.broadcast_to(cb_ref[...], (CONV_RB, D_CONV))
        for j in range(CONV_WIDTH):
            r0 = rb * CONV_RB + base + j
            acc = acc + cw_ref[j:j + 1, :] * uh_ref[r0:r0 + CONV_RB, :]
        for gi in range(CONV_GROUPS):
            gs = slice(gi * 128, (gi + 1) * 128)
            ug = acc[:, gs]
            mu = jnp.mean(ug, axis=-1, keepdims=True)
            dv = ug - mu
            var = jnp.mean(dv * dv, axis=-1, keepdims=True)
            yn = dv * lax.rsqrt(var + EPS) * lng_ref[:, gs] + lnb_ref[:, gs]
            y_ref[rb * CONV_RB:(rb + 1) * CONV_RB, gs] = _silu(yn).astype(BF16)
    uh_ref[0:HIST, :] = uh_ref[TL:TL + HIST, :]

    xh_ref[HIST:HIST + TL, :] = p_ref[:, 2 * D_CONV:2 * D_CONV + D_LRU]
    u = jnp.broadcast_to(lcb_ref[...], (TL, D_LRU))
    for j in range(LRU_CONV):
        r0 = HIST - (LRU_CONV - 1) + j
        u = u + lw_ref[j:j + 1, :] * xh_ref[r0:r0 + TL, :]
    xh_ref[0:HIST, :] = xh_ref[TL:TL + HIST, :]
    gates = _dot(u.astype(BF16), wax_ref[...]) + bax_ref[...]
    r = _sigmoid(gates[:, 0:D_LRU])
    ig = _sigmoid(gates[:, D_LRU:2 * D_LRU])
    lam = lam_ref[...]
    softplus_neg = jnp.maximum(-lam, 0.0) + jnp.log1p(jnp.exp(-jnp.abs(lam)))
    log_a = -LRU_C * r * softplus_neg
    a = jnp.exp(log_a)
    mult = jnp.sqrt(jnp.tanh(-log_a) * (a * a + 1.0))
    pos = lax.broadcasted_iota(jnp.int32, (TL, D_LRU), 0) + t * TL
    first = pos == 0
    bt = jnp.where(first, 1.0, mult) * (ig * u)
    a = jnp.where(first, 0.0, a)
    sub = lax.broadcasted_iota(jnp.int32, (TL, D_LRU), 0) & 7
    for s in (1, 2, 4):
        keep = sub >= s
        a_sh = jnp.where(keep, pltpu.roll(a, s, axis=0), 1.0)
        b_sh = jnp.where(keep, pltpu.roll(bt, s, axis=0), 0.0)
        bt = bt + a * b_sh
        a = a * a_sh
    sa_ref[...] = a
    sb_ref[...] = bt
    carry = hc_ref[0:1, :]
    for blk in range(TL // 8):
        rs = slice(blk * 8, blk * 8 + 8)
        hb = sb_ref[rs, :] + sa_ref[rs, :] * carry
        carry = hb[7:8, :]
        sb_ref[rs, :] = hb
    hc_ref[0:1, :] = carry
    dg = p_ref[:, 2 * D_CONV + D_LRU:2 * D_CONV + 2 * D_LRU]
    gelu = 0.5 * dg * (1.0 + jnp.tanh(0.7978845608028654 * (dg + 0.044715 * dg * dg * dg)))
    y_ref[:, D_CONV:D_CONV + D_LRU] = (gelu * sb_ref[...]).astype(BF16)

    out_ref[...] = h + _dot(y_ref[...], wout_ref[...])


def _odd_call(h3, g, win, cw, cb, lng, lnb, lw, lcb, wax, bax, lam, wout):
    row_spec = pl.BlockSpec((None, TL_MIX, D_MODEL), lambda b, t: (b, t, 0))
    return pl.pallas_call(
        _odd_kernel,
        out_shape=jax.ShapeDtypeStruct((BATCH, L_PAD, D_MODEL), F32),
        grid=(BATCH, L_PAD // TL_MIX),
        in_specs=[row_spec] + [_resident()] * 12,
        out_specs=row_spec,
        scratch_shapes=[
            pltpu.VMEM((HIST + TL_MIX, D_CONV), F32),
            pltpu.VMEM((HIST + TL_MIX, D_LRU), F32),
            pltpu.VMEM((8, D_LRU), F32),
            pltpu.VMEM((TL_MIX, D_IN_ODD), F32),
            pltpu.VMEM((TL_MIX, D_MODEL), BF16),
            pltpu.VMEM((TL_MIX, D_LRU), F32),
            pltpu.VMEM((TL_MIX, D_LRU), F32),
        ],
        compiler_params=pltpu.CompilerParams(
            dimension_semantics=("parallel", "arbitrary"), vmem_limit_bytes=VMEM_LIMIT),
    )(h3, g, win, cw, cb, lng, lnb, lw, lcb, wax, bax, lam, wout)


def _block_diag(w):
    g, n, _ = w.shape
    eye = jnp.eye(g, dtype=w.dtype)
    return jnp.einsum('gij,gh->gihj', w, eye).reshape(g * n, g * n)


def kernel(x, meta_tokens, ffn1_norm, ffn1_wg, ffn1_wu, ffn1_wd, mix_norm, ffn2_norm, ffn2_wg, ffn2_wu, ffn2_wd, w_in_even, pool_w, pool_scale, hgrn_lb_logits, hgrn_gnorm, w_out_even, w_in_odd, conv_w, conv_b, conv_ln_g, conv_ln_b, lru_conv_w, lru_conv_b, lru_wa, lru_ba, lru_wx, lru_bx, lru_lambda, w_out_odd, final_norm):
    bn = x.shape[0]
    meta = jnp.broadcast_to(meta_tokens.astype(F32)[None], (bn, N_META, D_MODEL))
    pad = jnp.zeros((bn, L_PAD - L_REAL, D_MODEL), F32)
    h = jnp.concatenate([meta, x, pad], axis=1)

    row = lambda a: a.reshape(1, -1).astype(F32)
    bf = lambda a: a.astype(BF16)

    def ffn_args(norm, wg, wu, wd, l):
        return (row(norm[l]), bf(wg[l]), bf(wu[l]), bf(wd[l]))

    tril = jnp.tril(jnp.ones((CHUNK, CHUNK), F32))

    h = _ffn_call(h.reshape(ROWS, D_MODEL), [ffn_args(ffn1_norm, ffn1_wg, ffn1_wu, ffn1_wd, 0)])
    h = _even_call(h.reshape(bn, L_PAD, D_MODEL), row(mix_norm[0]), bf(w_in_even[0]),
                   bf(_block_diag(pool_w[0])), row(pool_scale[0]), hgrn_lb_logits.astype(F32),
                   row(hgrn_gnorm[0]), bf(w_out_even[0]), tril)
    h = _ffn_call(h.reshape(ROWS, D_MODEL), [ffn_args(ffn2_norm, ffn2_wg, ffn2_wu, ffn2_wd, 0),
                                             ffn_args(ffn1_norm, ffn1_wg, ffn1_wu, ffn1_wd, 1)])
    wax = jnp.concatenate([_block_diag(lru_wa[0]), _block_diag(lru_wx[0])], axis=1)
    bax = jnp.concatenate([lru_ba[0], lru_bx[0]]).reshape(1, -1).astype(F32)
    h = _odd_call(h.reshape(bn, L_PAD, D_MODEL), row(mix_norm[1]), bf(w_in_odd[0]),
                  conv_w[0].astype(F32), row(conv_b[0]), row(conv_ln_g[0]), row(conv_ln_b[0]),
                  lru_conv_w[0].astype(F32), row(lru_conv_b[0]), bf(wax), bax, row(lru_lambda[0]),
                  bf(w_out_odd[0]))
    h = _ffn_call(h.reshape(ROWS, D_MODEL), [ffn_args(ffn2_norm, ffn2_wg, ffn2_wu, ffn2_wd, 1)],
                  final_g=row(final_norm))
    return h.reshape(bn, L_PAD, D_MODEL)[:, N_META:L_REAL]
```

```python
import functools

import jax
import jax.numpy as jnp
from jax import lax
from jax.experimental import pallas as pl
from jax.experimental.pallas import tpu as pltpu

F32 = jnp.float32
BF16 = jnp.bfloat16

D_MODEL = 1024
BATCH = 8
SEQ = 4096
N_META = 16
D_FF = 2816
EPS = 1e-6
D_POOL = 256
HG_HEAD = 128
D_HGRN = 768
HG_HEADS = 6
CHUNK = 64
D_IN_EVEN = D_POOL + 4 * D_HGRN
D_CONV = 512
CONV_WIDTH = 31
CONV_GROUPS = 4
D_LRU = 512
LRU_CONV = 4
LRU_C = 8.0
D_IN_ODD = 2 * D_CONV + 2 * D_LRU
LANES = 128
SUBLANES = 8

L_REAL = N_META + SEQ
L_PAD = ((L_REAL + CHUNK - 1) // CHUNK) * CHUNK
ROWS = BATCH * L_PAD

TM_FFN = 320
FF_CHUNK = 1408
TL_MIX = 320
N_CHUNKS = TL_MIX // CHUNK
TILES_PER_SEQ = L_PAD // TL_MIX
N_TILES = ROWS // TL_MIX
HIST = 32
VMEM_LIMIT = 56 * 1024 * 1024


def _rms(x, g):
    ms = jnp.mean(x * x, axis=-1, keepdims=True)
    return x * lax.rsqrt(ms + EPS) * g


def _sigmoid(x):
    return 1.0 / (1.0 + jnp.exp(-x))


def _silu(x):
    return x * _sigmoid(x)


def _dot(a, b):
    return jnp.dot(a, b, preferred_element_type=F32)


def _dot_nt(a, b):
    return lax.dot_general(a, b, (((1,), (1,)), ((), ())), preferred_element_type=F32)


def _dot_tn(a, b):
    return lax.dot_general(a, b, (((0,), (0,)), ((), ())), preferred_element_type=F32)


def _swiglu(xn, wg_ref, wu_ref, wd_ref):
    acc = None
    for c in range(D_FF // FF_CHUNK):
        sl = slice(c * FF_CHUNK, (c + 1) * FF_CHUNK)
        gate = _dot(xn, wg_ref[:, sl])
        up = _dot(xn, wu_ref[:, sl])
        hid = (_silu(gate) * up).astype(BF16)
        part = _dot(hid, wd_ref[sl, :])
        acc = part if acc is None else acc + part
    return acc


def _ffn_kernel(n_ffn, final, *refs):
    h_ref = refs[0]
    out_ref = refs[-1]
    h = h_ref[...]
    for i in range(n_ffn):
        g_ref, wg_ref, wu_ref, wd_ref = refs[1 + 4 * i: 5 + 4 * i]
        xn = _rms(h, g_ref[...]).astype(BF16)
        h = h + 0.5 * _swiglu(xn, wg_ref, wu_ref, wd_ref)
    if final:
        h = _rms(h, refs[1 + 4 * n_ffn][...])
    out_ref[...] = h


def _resident():
    return pl.BlockSpec(memory_space=pltpu.VMEM)


def _ffn_call(h, ffns, final_g=None):
    args = [h]
    in_specs = [pl.BlockSpec((TM_FFN, D_MODEL), lambda i: (i, 0))]
    for g, wg, wu, wd in ffns:
        args += [g, wg, wu, wd]
        in_specs += [_resident()] * 4
    if final_g is not None:
        args.append(final_g)
        in_specs.append(_resident())
    return pl.pallas_call(
        functools.partial(_ffn_kernel, len(ffns), final_g is not None),
        out_shape=jax.ShapeDtypeStruct((ROWS, D_MODEL), F32),
        grid=(ROWS // TM_FFN,),
        in_specs=in_specs,
        out_specs=pl.BlockSpec((TM_FFN, D_MODEL), lambda i: (i, 0)),
        compiler_params=pltpu.CompilerParams(
            dimension_semantics=("parallel",), vmem_limit_bytes=VMEM_LIMIT),
        name="ffn",
    )(*args)


def _mixer_steps(h_ref, hn_ref, g_ref, win_ref, pa_ref, pb_ref, reset, mix):
    s = pl.program_id(0)
    g = g_ref[...]
    d_in = win_ref.shape[1]
    n_col_tiles = d_in // (2 * LANES)
    bounds = [2 * LANES * ((n_col_tiles * i) // N_CHUNKS) for i in range(N_CHUNKS + 1)]

    @pl.when(s == 0)
    def _():
        pa_ref[...] = _dot(_rms(h_ref[...], g).astype(BF16), win_ref[...])

    @pl.when(s % TILES_PER_SEQ == 0)
    def _():
        reset()

    def step(cur_ref, nxt_ref):
        un = _rms(hn_ref[...], g).astype(BF16)

        def project_piece(c):
            lo, hi = bounds[c], bounds[c + 1]
            if hi > lo:
                nxt_ref[:, lo:hi] = _dot(un, win_ref[:, lo:hi])

        mix(cur_ref, project_piece)

    @pl.when(s % 2 == 0)
    def _():
        step(pa_ref, pb_ref)

    @pl.when(s % 2 == 1)
    def _():
        step(pb_ref, pa_ref)


def _mixer_call(body, name, h, consts, d_in, scratch):
    cur_spec = pl.BlockSpec((TL_MIX, D_MODEL), lambda s: (s, 0))
    nxt_spec = pl.BlockSpec((TL_MIX, D_MODEL), lambda s: (jnp.minimum(s + 1, N_TILES - 1), 0))
    return pl.pallas_call(
        body,
        out_shape=jax.ShapeDtypeStruct((ROWS, D_MODEL), F32),
        grid=(N_TILES,),
        in_specs=[cur_spec, nxt_spec] + [_resident()] * len(consts),
        out_specs=cur_spec,
        scratch_shapes=[pltpu.VMEM((TL_MIX, d_in), F32),
                        pltpu.VMEM((TL_MIX, d_in), F32),
                        pltpu.VMEM((TL_MIX, D_MODEL), BF16),
                        ] + scratch,
        compiler_params=pltpu.CompilerParams(
            dimension_semantics=("arbitrary",), vmem_limit_bytes=VMEM_LIMIT),
        name=name,
    )(h, h, *consts)


def _even_kernel(h_ref, hn_ref, g_ref, win_ref, poolw_ref, pscale_ref, lbl_ref, gn_ref, wout_ref, tril_ref,
                 out_ref, pa_ref, pb_ref, y_ref, st_ref, hist_ref):
    s = pl.program_id(0)

    def reset():
        st_ref[...] = jnp.zeros_like(st_ref)
        hist_ref[...] = jnp.zeros_like(hist_ref)

    def mix(p_ref, project_piece):
        lbl = lbl_ref[...]
        mx = jnp.maximum(lbl[0:1, :], lbl[1:2, :])
        e0 = jnp.exp(lbl[0:1, :] - mx)
        e1 = jnp.exp(lbl[1:2, :] - mx)
        lb = e0 / (e0 + e1)
        log_lb = jnp.log(lb)
        log_1mlb = jnp.log1p(-lb)
        one_mlb = 1.0 - lb
        gn = gn_ref[...]
        tril = tril_ref[...]
        causal = tril > 0.5
        pscale = pscale_ref[...]
        lane = lax.broadcasted_iota(jnp.int32, (CHUNK, D_POOL), 1)
        win = jnp.where(lane < 64, 2.0, jnp.where(lane < 128, 4.0, jnp.where(lane < 192, 8.0, 16.0)))
        row1 = lax.broadcasted_iota(jnp.int32, (CHUNK, D_POOL), 0) + ((s % TILES_PER_SEQ) * TL_MIX + 1)

        for c in range(N_CHUNKS):
            project_piece(c)
            r0 = c * CHUNK
            prow = slice(r0, r0 + CHUNK)

            if c == 0:
                w = jnp.concatenate([hist_ref[HIST - 16:HIST, :], p_ref[0:CHUNK, 0:D_POOL]], axis=0)
            else:
                w = p_ref[r0 - 16:r0 + CHUNK, 0:D_POOL]
            s2 = w + pltpu.roll(w, 1, axis=0)
            s4 = s2 + pltpu.roll(s2, 2, axis=0)
            s8 = s4 + pltpu.roll(s4, 4, axis=0)
            s16 = s8 + pltpu.roll(s8, 8, axis=0)
            x = w[16:]
            ssel = jnp.where(lane < 64, s2[16:],
                             jnp.where(lane < 128, s4[16:], jnp.where(lane < 192, s8[16:], s16[16:])))
            cnt = jnp.minimum((row1 + r0).astype(F32), win)
            mixed = ssel / cnt - x
            ya = _dot(mixed.astype(BF16), poolw_ref[...]) * pscale
            y_ref[prow, 0:D_POOL] = ya.astype(BF16)

            q = _silu(p_ref[prow, D_POOL:D_POOL + D_HGRN])
            z = p_ref[prow, D_POOL + D_HGRN:D_POOL + 2 * D_HGRN]
            v = p_ref[prow, D_POOL + 2 * D_HGRN:D_POOL + 3 * D_HGRN].astype(BF16)
            ez = jnp.exp(-jnp.abs(z))
            wz = 1.0 + ez
            log_sig = jnp.minimum(z, 0.0) - jnp.log(wz)
            rz = 1.0 / wz
            k = one_mlb * jnp.where(z >= 0, ez * rz, rz)
            bb = log_1mlb + log_sig
            log_f = jnp.maximum(log_lb, bb) + jnp.log(1.0 + jnp.exp(-jnp.abs(log_lb - bb)))
            lf_hi = log_f.astype(BF16)
            lf_lo = (log_f - lf_hi.astype(F32)).astype(BF16)
            b = _dot(tril, lf_hi) + _dot(tril, lf_lo)
            b_end = b[CHUNK - 1:CHUNK, :]
            mid = 0.5 * b_end
            e_mid = jnp.exp(mid)
            bm = b - mid
            qt = q * jnp.exp(bm)
            kt = k * jnp.exp(-bm)
            qi = (qt * e_mid).astype(BF16)
            ks = (kt * e_mid).astype(BF16)
            qt = qt.astype(BF16)
            kt = kt.astype(BF16)
            dec = jnp.exp(b_end)
            for hd in range(HG_HEADS):
                hs = slice(hd * HG_HEAD, (hd + 1) * HG_HEAD)
                a = jnp.where(causal, _dot_nt(qt[:, hs], kt[:, hs]), 0.0)
                st = st_ref[hd]
                o = _dot(a.astype(BF16), v[:, hs]) + _dot_nt(qi[:, hs], st.astype(BF16))
                st_ref[hd] = st * dec[:, hs] + _dot_tn(v[:, hs], ks[:, hs])
                o = o * lax.rsqrt(jnp.mean(o * o, axis=-1, keepdims=True) + EPS) * gn
                gcol = D_POOL + 3 * D_HGRN + hd * HG_HEAD
                gg = p_ref[prow, gcol:gcol + HG_HEAD]
                y_ref[prow, D_POOL + hd * HG_HEAD:D_POOL + (hd + 1) * HG_HEAD] = (o * _silu(gg)).astype(BF16)

        hist_ref[...] = p_ref[TL_MIX - HIST:TL_MIX, 0:D_POOL]
        out_ref[...] = h_ref[...] + _dot(y_ref[...], wout_ref[...])

    _mixer_steps(h_ref, hn_ref, g_ref, win_ref, pa_ref, pb_ref, reset, mix)


def _even_call(h, g, win, poolw, pscale, lbl, gn, wout, tril):
    consts = (g, win, poolw, pscale, lbl, gn, wout, tril)
    scratch = [pltpu.VMEM((HG_HEADS, HG_HEAD, HG_HEAD), F32),
               pltpu.VMEM((HIST, D_POOL), F32)]
    return _mixer_call(_even_kernel, "even_mixer", h, consts, D_IN_EVEN, scratch)


def _odd_kernel(h_ref, hn_ref, g_ref, win_ref, cw_ref, cb_ref, lng_ref, lnb_ref, lw_ref, lcb_ref, wax_ref,
                bax_ref, lam_ref, wout_ref, out_ref, pa_ref, pb_ref, y_ref, hc_ref, hu_ref, hx_ref):
    s = pl.program_id(0)
    XB = 2 * D_CONV
    GC = 2 * D_CONV + D_LRU

    def reset():
        hc_ref[...] = jnp.zeros_like(hc_ref)
        hu_ref[...] = jnp.zeros_like(hu_ref)
        hx_ref[...] = jnp.zeros_like(hx_ref)

    def mix(p_ref, project_piece):
        lam = lam_ref[...]
        neg_c_softplus = -LRU_C * (jnp.maximum(-lam, 0.0) + jnp.log1p(jnp.exp(-jnp.abs(lam))))
        bax = bax_ref[...]
        sub = lax.broadcasted_iota(jnp.int32, (CHUNK, D_LRU), 0) & (SUBLANES - 1)
        row0 = lax.broadcasted_iota(jnp.int32, (CHUNK, D_LRU), 0) + (s % TILES_PER_SEQ) * TL_MIX
        carry = hc_ref[0:1, :]

        for c in range(N_CHUNKS):
            project_piece(c)
            r0 = c * CHUNK
            prow = slice(r0, r0 + CHUNK)

            p_ref[prow, 0:D_CONV] = p_ref[prow, 0:D_CONV] * _sigmoid(p_ref[prow, D_CONV:2 * D_CONV])
            for gi in range(CONV_GROUPS):
                gs = slice(gi * LANES, (gi + 1) * LANES)
                if c == 0:
                    uw = jnp.concatenate([hu_ref[:, gs], p_ref[0:CHUNK, gs]], axis=0)
                else:
                    uw = p_ref[r0 - HIST:r0 + CHUNK, gs]
                acc = None
                for r in range(SUBLANES):
                    part = None
                    for a in range((CONV_WIDTH - 1 - r) // SUBLANES + 1):
                        j = CONV_WIDTH - 1 - (SUBLANES * a + r)
                        lo = HIST - SUBLANES * (a + 1)
                        term = cw_ref[j:j + 1, gs] * uw[lo:lo + CHUNK + SUBLANES]
                        part = term if part is None else part + term
                    if r:
                        part = pltpu.roll(part, r, axis=0)
                    acc = part if acc is None else acc + part
                ug = acc[SUBLANES:] + cb_ref[:, gs]
                mu = jnp.mean(ug, axis=-1, keepdims=True)
                dv = ug - mu
                var = jnp.mean(dv * dv, axis=-1, keepdims=True)
                yn = dv * lax.rsqrt(var + EPS) * lng_ref[:, gs] + lnb_ref[:, gs]
                y_ref[prow, gs] = _silu(yn).astype(BF16)

            if c == 0:
                xw = jnp.concatenate([hx_ref[HIST - SUBLANES:HIST, :], p_ref[0:CHUNK, XB:XB + D_LRU]], axis=0)
            else:
                xw = p_ref[r0 - SUBLANES:r0 + CHUNK, XB:XB + D_LRU]
            u = lw_ref[LRU_CONV - 1:LRU_CONV, :] * xw
            for d in range(1, LRU_CONV):
                u = u + lw_ref[LRU_CONV - 1 - d:LRU_CONV - d, :] * pltpu.roll(xw, d, axis=0)
            u = u[SUBLANES:] + lcb_ref[...]
            gates = _dot(u.astype(BF16), wax_ref[...]) + bax
            r = _sigmoid(gates[:, 0:D_LRU])
            ig = _sigmoid(gates[:, D_LRU:2 * D_LRU])
            log_a = r * neg_c_softplus
            a = jnp.exp(log_a)
            mult = jnp.sqrt(jnp.tanh(-log_a) * (a * a + 1.0))
            first = (row0 + r0) == 0
            bt = jnp.where(first, 1.0, mult) * (ig * u)
            a = jnp.where(first, 0.0, a)
            for sh in (1, 2, 4):
                keep = sub >= sh
                a_sh = jnp.where(keep, pltpu.roll(a, sh, axis=0), 1.0)
                b_sh = jnp.where(keep, pltpu.roll(bt, sh, axis=0), 0.0)
                bt = bt + a * b_sh
                a = a * a_sh
            hs = []
            for blk in range(CHUNK // SUBLANES):
                rs = slice(blk * SUBLANES, (blk + 1) * SUBLANES)
                hb = bt[rs, :] + a[rs, :] * carry
                carry = hb[SUBLANES - 1:SUBLANES, :]
                hs.append(hb)
            hseq = jnp.concatenate(hs, axis=0)
            dg = p_ref[prow, GC:GC + D_LRU]
            gelu = 0.5 * dg * (1.0 + jnp.tanh(0.7978845608028654 * (dg + 0.044715 * dg * dg * dg)))
            y_ref[prow, D_CONV:D_CONV + D_LRU] = (gelu * hseq).astype(BF16)

        hc_ref[0:1, :] = carry
        hu_ref[...] = p_ref[TL_MIX - HIST:TL_MIX, 0:D_CONV]
        hx_ref[...] = p_ref[TL_MIX - HIST:TL_MIX, XB:XB + D_LRU]
        out_ref[...] = h_ref[...] + _dot(y_ref[...], wout_ref[...])

    _mixer_steps(h_ref, hn_ref, g_ref, win_ref, pa_ref, pb_ref, reset, mix)


def _odd_call(h, g, win, cw, cb, lng, lnb, lw, lcb, wax, bax, lam, wout):
    consts = (g, win, cw, cb, lng, lnb, lw, lcb, wax, bax, lam, wout)
    scratch = [pltpu.VMEM((SUBLANES, D_LRU), F32),
               pltpu.VMEM((HIST, D_CONV), F32),
               pltpu.VMEM((HIST, D_LRU), F32)]
    return _mixer_call(_odd_kernel, "odd_mixer", h, consts, D_IN_ODD, scratch)


def _block_diag(w):
    g, n, _ = w.shape
    eye = jnp.eye(g, dtype=w.dtype)
    return jnp.einsum('gij,gh->gihj', w, eye).reshape(g * n, g * n)


def kernel(x, meta_tokens, ffn1_norm, ffn1_wg, ffn1_wu, ffn1_wd, mix_norm, ffn2_norm, ffn2_wg, ffn2_wu, ffn2_wd, w_in_even, pool_w, pool_scale, hgrn_lb_logits, hgrn_gnorm, w_out_even, w_in_odd, conv_w, conv_b, conv_ln_g, conv_ln_b, lru_conv_w, lru_conv_b, lru_wa, lru_ba, lru_wx, lru_bx, lru_lambda, w_out_odd, final_norm):
    bn = x.shape[0]
    meta = jnp.broadcast_to(meta_tokens.astype(F32)[None], (bn, N_META, D_MODEL))
    pad = jnp.zeros((bn, L_PAD - L_REAL, D_MODEL), F32)
    h = jnp.concatenate([meta, x, pad], axis=1).reshape(ROWS, D_MODEL)

    row = lambda a: a.reshape(1, -1).astype(F32)
    bf = lambda a: a.astype(BF16)

    def ffn_args(norm, wg, wu, wd, l):
        return (row(norm[l]), bf(wg[l]), bf(wu[l]), bf(wd[l]))

    tril = jnp.tril(jnp.ones((CHUNK, CHUNK), BF16))

    h = _ffn_call(h, [ffn_args(ffn1_norm, ffn1_wg, ffn1_wu, ffn1_wd, 0)])
    h = _even_call(h, row(mix_norm[0]), bf(w_in_even[0]), bf(_block_diag(pool_w[0])), row(pool_scale[0]),
                   hgrn_lb_logits.astype(F32), row(hgrn_gnorm[0]), bf(w_out_even[0]), tril)
    h = _ffn_call(h, [ffn_args(ffn2_norm, ffn2_wg, ffn2_wu, ffn2_wd, 0),
                      ffn_args(ffn1_norm, ffn1_wg, ffn1_wu, ffn1_wd, 1)])
    wax = jnp.concatenate([_block_diag(lru_wa[0]), _block_diag(lru_wx[0])], axis=1)
    bax = jnp.concatenate([lru_ba[0], lru_bx[0]]).reshape(1, -1).astype(F32)
    h = _odd_call(h, row(mix_norm[1]), bf(w_in_odd[0]), conv_w[0].astype(F32), row(conv_b[0]),
                  row(conv_ln_g[0]), row(conv_ln_b[0]), lru_conv_w[0].astype(F32), row(lru_conv_b[0]),
                  bf(wax), bax, row(lru_lambda[0]), bf(w_out_odd[0]))
    h = _ffn_call(h, [ffn_args(ffn2_norm, ffn2_wg, ffn2_wu, ffn2_wd, 1)], final_g=row(final_norm))
    return h.reshape(bn, L_PAD, D_MODEL)[:, N_META:L_REAL]
```

```python
import functools

import jax
import jax.numpy as jnp
from jax import lax
from jax.experimental import pallas as pl
from jax.experimental.pallas import tpu as pltpu

F32 = jnp.float32
BF16 = jnp.bfloat16

D_MODEL = 1024
BATCH = 8
SEQ = 4096
N_META = 16
D_FF = 2816
EPS = 1e-6
D_POOL = 256
HG_HEAD = 128
D_HGRN = 768
HG_HEADS = 6
CHUNK = 64
D_IN_EVEN = D_POOL + 4 * D_HGRN
D_CONV = 512
CONV_WIDTH = 31
CONV_GROUPS = 4
D_LRU = 512
LRU_CONV = 4
LRU_C = 8.0
D_IN_ODD = 2 * D_CONV + 2 * D_LRU
LANES = 128
SUBLANES = 8
MXU_COLS = 256

HEAD = CHUNK
N_PAD = HEAD - N_META
L_PAD = HEAD + SEQ
ROWS = BATCH * L_PAD

TM_FFN = 320
TM_OUT = 512
FF_CHUNK = 1408
FF_PIECE = 256
TL_MIX = 320
N_CHUNKS = TL_MIX // CHUNK
TILES_PER_SEQ = L_PAD // TL_MIX
N_TILES = ROWS // TL_MIX
HIST = 32
VMEM_LIMIT = 58 * 1024 * 1024


def _rms(x, g):
    ms = jnp.mean(x * x, axis=-1, keepdims=True)
    return x * lax.rsqrt(ms + EPS) * g


def _sigmoid(x):
    return 1.0 / (1.0 + jnp.exp(-x))


def _silu(x):
    return x * _sigmoid(x)


def _dot(a, b):
    return jnp.dot(a, b, preferred_element_type=F32)


def _dot_nt(a, b):
    return lax.dot_general(a, b, (((1,), (1,)), ((), ())), preferred_element_type=F32)


def _dot_tn(a, b):
    return lax.dot_general(a, b, (((0,), (0,)), ((), ())), preferred_element_type=F32)


def _resident():
    return pl.BlockSpec(memory_space=pltpu.VMEM)


def _swiglu(xn, wg_ref, wu_ref, wd_ref):
    acc = None
    for c in range(D_FF // FF_CHUNK):
        sl = slice(c * FF_CHUNK, (c + 1) * FF_CHUNK)
        gate = _dot(xn, wg_ref[:, sl])
        up = _dot(xn, wu_ref[:, sl])
        hid = (_silu(gate) * up).astype(BF16)
        part = _dot(hid, wd_ref[sl, :])
        acc = part if acc is None else acc + part
    return acc


def _ffn_kernel(final, h_ref, g_ref, wg_ref, wu_ref, wd_ref, *rest):
    out_ref = rest[-1]
    h = h_ref[...]
    xn = _rms(h, g_ref[...]).astype(BF16)
    h = h + 0.5 * _swiglu(xn, wg_ref, wu_ref, wd_ref)
    if final:
        h = _rms(h, rest[0][...])
    out_ref[...] = h


def _ffn_call(h, ffn):
    spec = pl.BlockSpec((TM_FFN, D_MODEL), lambda i: (i, 0))
    return pl.pallas_call(
        functools.partial(_ffn_kernel, False),
        out_shape=jax.ShapeDtypeStruct((ROWS, D_MODEL), F32),
        grid=(ROWS // TM_FFN,),
        in_specs=[spec] + [_resident()] * 4,
        out_specs=spec,
        compiler_params=pltpu.CompilerParams(
            dimension_semantics=("parallel",), vmem_limit_bytes=VMEM_LIMIT),
        name="ffn",
    )(h, *ffn)


def _ffn_final_call(h, ffn, final_g):
    per_seq = SEQ // TM_OUT
    in_spec = pl.BlockSpec((pl.Element(TM_OUT), pl.Element(D_MODEL)),
                           lambda i: (CHUNK * ((i // per_seq) * (L_PAD // CHUNK) + HEAD // CHUNK
                                               + (i % per_seq) * (TM_OUT // CHUNK)), 0))
    return pl.pallas_call(
        functools.partial(_ffn_kernel, True),
        out_shape=jax.ShapeDtypeStruct((BATCH * SEQ, D_MODEL), F32),
        grid=(BATCH * per_seq,),
        in_specs=[in_spec] + [_resident()] * 5,
        out_specs=pl.BlockSpec((TM_OUT, D_MODEL), lambda i: (i, 0)),
        compiler_params=pltpu.CompilerParams(
            dimension_semantics=("parallel",), vmem_limit_bytes=VMEM_LIMIT),
        name="ffn_final",
    )(h, *ffn, final_g)


def _prepare_items(raw, ffn_refs, g_mix, win_ref, acc_ref, h1_ref, un_ref, p_ref):
    g1_ref, wg_ref, wu_ref, wd_ref = ffn_refs
    xn = _rms(raw, g1_ref[...]).astype(BF16)
    n_ff = D_FF // FF_PIECE
    items = []
    for c in range(n_ff):
        def ff_piece(c=c):
            sl = slice(c * FF_PIECE, (c + 1) * FF_PIECE)
            hid = (_silu(_dot(xn, wg_ref[:, sl])) * _dot(xn, wu_ref[:, sl])).astype(BF16)
            part = _dot(hid, wd_ref[sl, :])
            if c == 0:
                acc_ref[...] = part
            else:
                acc_ref[...] += part
            if c == n_ff - 1:
                h1 = raw + 0.5 * acc_ref[...]
                h1_ref[...] = h1
                un_ref[...] = _rms(h1, g_mix).astype(BF16)
        items.append((3 * D_MODEL * FF_PIECE, ff_piece))
    for j in range(win_ref.shape[1] // MXU_COLS):
        def proj_piece(j=j):
            sl = slice(j * MXU_COLS, (j + 1) * MXU_COLS)
            p_ref[:, sl] = _dot(un_ref[...], win_ref[:, sl])
        items.append((D_MODEL * MXU_COLS, proj_piece))
    return items


def _slots(items, n):
    total = sum(cost for cost, _ in items)
    groups = [[] for _ in range(n)]
    done = 0
    for cost, fn in items:
        groups[min(n - 1, (done * n) // total)].append(fn)
        done += cost
    return groups


def _fused_steps(raw_fn, ffn_refs, g_mix_ref, win_ref, bufs_a, bufs_b, acc_ref, un_ref, reset, mix):
    s = pl.program_id(0)

    @pl.when(s == 0)
    def _():
        for ref in bufs_a + bufs_b:
            ref[...] = jnp.zeros_like(ref)
        reset()

    @pl.when(s % TILES_PER_SEQ == 1 % TILES_PER_SEQ)
    def _():
        reset()

    def step(prep, cur):
        h1n_ref, pn_ref = prep
        h1_ref, p_ref = cur
        items = _prepare_items(raw_fn(), ffn_refs, g_mix_ref[...], win_ref, acc_ref, h1n_ref, un_ref, pn_ref)
        groups = _slots(items, N_CHUNKS)

        def work(c):
            for fn in groups[c]:
                fn()

        mix(p_ref, h1_ref, work)

    @pl.when(s % 2 == 0)
    def _():
        step(bufs_a, bufs_b)

    @pl.when(s % 2 == 1)
    def _():
        step(bufs_b, bufs_a)


def _fused_call(body, name, first_arg, first_spec, consts, d_in, scratch):
    out_spec = pl.BlockSpec((TL_MIX, D_MODEL), lambda s: (jnp.maximum(s - 1, 0), 0))
    return pl.pallas_call(
        body,
        out_shape=jax.ShapeDtypeStruct((ROWS, D_MODEL), F32),
        grid=(N_TILES + 1,),
        in_specs=[first_spec] + [_resident()] * len(consts),
        out_specs=out_spec,
        scratch_shapes=[pltpu.VMEM((TL_MIX, D_MODEL), F32),
                        pltpu.VMEM((TL_MIX, d_in), F32),
                        pltpu.VMEM((TL_MIX, D_MODEL), F32),
                        pltpu.VMEM((TL_MIX, d_in), F32),
                        pltpu.VMEM((TL_MIX, D_MODEL), F32),
                        pltpu.VMEM((TL_MIX, D_MODEL), BF16),
                        pltpu.VMEM((TL_MIX, D_MODEL), BF16),
                        ] + scratch,
        compiler_params=pltpu.CompilerParams(
            dimension_semantics=("arbitrary",), vmem_limit_bytes=VMEM_LIMIT),
        name=name,
    )(first_arg, *consts)


def _mixed_tile_row0():
    s = pl.program_id(0)
    return ((s + TILES_PER_SEQ - 1) % TILES_PER_SEQ) * TL_MIX


def _even_kernel(x_ref, head_ref, g1_ref, wg_ref, wu_ref, wd_ref, g_ref, win_ref, poolw_ref, pscale_ref, lbl_ref,
                 gn_ref, wout_ref, tril_ref, out_ref, h1a_ref, pa_ref, h1b_ref, pb_ref, acc_ref, un_ref, y_ref,
                 st_ref, hist_ref):
    s = pl.program_id(0)

    def raw_tile():
        blk = x_ref[...]
        first = jnp.minimum(s, N_TILES - 1) % TILES_PER_SEQ == 0
        spliced = jnp.concatenate([head_ref[...], blk[0:TL_MIX - HEAD]], axis=0)
        return jnp.where(first, spliced, blk)

    def reset():
        st_ref[...] = jnp.zeros_like(st_ref)
        hist_ref[...] = jnp.zeros_like(hist_ref)

    def mix(p_ref, h1_ref, work):
        lbl = lbl_ref[...]
        mx = jnp.maximum(lbl[0:1, :], lbl[1:2, :])
        e0 = jnp.exp(lbl[0:1, :] - mx)
        e1 = jnp.exp(lbl[1:2, :] - mx)
        lb = e0 / (e0 + e1)
        log_lb = jnp.log(lb)
        log_1mlb = jnp.log1p(-lb)
        one_mlb = 1.0 - lb
        gn = gn_ref[...]
        tril = tril_ref[...]
        causal = tril > 0.5
        pscale = pscale_ref[...]
        lane = lax.broadcasted_iota(jnp.int32, (CHUNK, D_POOL), 1)
        win = jnp.where(lane < 64, 2.0, jnp.where(lane < 128, 4.0, jnp.where(lane < 192, 8.0, 16.0)))
        pos = lax.broadcasted_iota(jnp.int32, (CHUNK, D_POOL), 0) + (_mixed_tile_row0() + 1 - N_PAD)

        for c in range(N_CHUNKS):
            work(c)
            r0 = c * CHUNK
            prow = slice(r0, r0 + CHUNK)

            if c == 0:
                w = jnp.concatenate([hist_ref[HIST - 16:HIST, :], p_ref[0:CHUNK, 0:D_POOL]], axis=0)
            else:
                w = p_ref[r0 - 16:r0 + CHUNK, 0:D_POOL]
            s2 = w + pltpu.roll(w, 1, axis=0)
            s4 = s2 + pltpu.roll(s2, 2, axis=0)
            s8 = s4 + pltpu.roll(s4, 4, axis=0)
            s16 = s8 + pltpu.roll(s8, 8, axis=0)
            x = w[16:]
            ssel = jnp.where(lane < 64, s2[16:],
                             jnp.where(lane < 128, s4[16:], jnp.where(lane < 192, s8[16:], s16[16:])))
            cnt = jnp.minimum(jnp.maximum(pos + r0, 1).astype(F32), win)
            mixed = ssel / cnt - x
            ya = _dot(mixed.astype(BF16), poolw_ref[...]) * pscale
            y_ref[prow, 0:D_POOL] = ya.astype(BF16)

            q = _silu(p_ref[prow, D_POOL:D_POOL + D_HGRN])
            z = p_ref[prow, D_POOL + D_HGRN:D_POOL + 2 * D_HGRN]
            v = p_ref[prow, D_POOL + 2 * D_HGRN:D_POOL + 3 * D_HGRN].astype(BF16)
            ez = jnp.exp(-jnp.abs(z))
            wz = 1.0 + ez
            log_sig = jnp.minimum(z, 0.0) - jnp.log(wz)
            rz = 1.0 / wz
            k = one_mlb * jnp.where(z >= 0, ez * rz, rz)
            bb = log_1mlb + log_sig
            log_f = jnp.maximum(log_lb, bb) + jnp.log(1.0 + jnp.exp(-jnp.abs(log_lb - bb)))
            lf_hi = log_f.astype(BF16)
            lf_lo = (log_f - lf_hi.astype(F32)).astype(BF16)
            b = _dot(tril, lf_hi) + _dot(tril, lf_lo)
            b_end = b[CHUNK - 1:CHUNK, :]
            mid = 0.5 * b_end
            e_mid = jnp.exp(mid)
            bm = b - mid
            qt = q * jnp.exp(bm)
            kt = k * jnp.exp(-bm)
            qi = (qt * e_mid).astype(BF16)
            ks = (kt * e_mid).astype(BF16)
            qt = qt.astype(BF16)
            kt = kt.astype(BF16)
            dec = jnp.exp(b_end)
            for hd in range(HG_HEADS):
                hs = slice(hd * HG_HEAD, (hd + 1) * HG_HEAD)
                a = jnp.where(causal, _dot_nt(qt[:, hs], kt[:, hs]), 0.0)
                st = st_ref[hd]
                o = _dot(a.astype(BF16), v[:, hs]) + _dot_nt(qi[:, hs], st.astype(BF16))
                st_ref[hd] = st * dec[:, hs] + _dot_tn(v[:, hs], ks[:, hs])
                o = o * lax.rsqrt(jnp.mean(o * o, axis=-1, keepdims=True) + EPS) * gn
                gcol = D_POOL + 3 * D_HGRN + hd * HG_HEAD
                gg = p_ref[prow, gcol:gcol + HG_HEAD]
                y_ref[prow, D_POOL + hd * HG_HEAD:D_POOL + (hd + 1) * HG_HEAD] = (o * _silu(gg)).astype(BF16)

        hist_ref[...] = p_ref[TL_MIX - HIST:TL_MIX, 0:D_POOL]
        out_ref[...] = h1_ref[...] + _dot(y_ref[...], wout_ref[...])

    _fused_steps(raw_tile, (g1_ref, wg_ref, wu_ref, wd_ref), g_ref, win_ref, (h1a_ref, pa_ref), (h1b_ref, pb_ref),
                 acc_ref, un_ref, reset, mix)


def _x_tile_offset(s):
    n = jnp.minimum(s, N_TILES - 1)
    return CHUNK * ((n // TILES_PER_SEQ) * (SEQ // CHUNK) + jnp.maximum((n % TILES_PER_SEQ) * N_CHUNKS - 1, 0))


def _even_call(x_flat, head, ffn, g, win, poolw, pscale, lbl, gn, wout, tril):
    consts = (head,) + ffn + (g, win, poolw, pscale, lbl, gn, wout, tril)
    x_spec = pl.BlockSpec((pl.Element(TL_MIX), pl.Element(D_MODEL)), lambda s: (_x_tile_offset(s), 0))
    scratch = [pltpu.VMEM((HG_HEADS, HG_HEAD, HG_HEAD), F32),
               pltpu.VMEM((HIST, D_POOL), F32)]
    return _fused_call(_even_kernel, "ffn_even_mixer", x_flat, x_spec, consts, D_IN_EVEN, scratch)


def _odd_kernel(h_ref, g1_ref, wg_ref, wu_ref, wd_ref, g_ref, win_ref, cw_ref, cb_ref, lng_ref, lnb_ref, lw_ref,
                lcb_ref, wax_ref, bax_ref, lam_ref, wout_ref, out_ref, h1a_ref, pa_ref, h1b_ref, pb_ref, acc_ref,
                un_ref, y_ref, hc_ref, hu_ref, hx_ref):
    XB = 2 * D_CONV
    GC = 2 * D_CONV + D_LRU

    def reset():
        hc_ref[...] = jnp.zeros_like(hc_ref)
        hu_ref[...] = jnp.zeros_like(hu_ref)
        hx_ref[...] = jnp.zeros_like(hx_ref)

    def mix(p_ref, h1_ref, work):
        lam = lam_ref[...]
        neg_c_softplus = -LRU_C * (jnp.maximum(-lam, 0.0) + jnp.log1p(jnp.exp(-jnp.abs(lam))))
        bax = bax_ref[...]
        sub = lax.broadcasted_iota(jnp.int32, (CHUNK, D_LRU), 0) & (SUBLANES - 1)
        row0 = lax.broadcasted_iota(jnp.int32, (CHUNK, D_LRU), 0) + _mixed_tile_row0()
        carry = hc_ref[0:1, :]

        for c in range(N_CHUNKS):
            work(c)
            r0 = c * CHUNK
            prow = slice(r0, r0 + CHUNK)

            p_ref[prow, 0:D_CONV] = p_ref[prow, 0:D_CONV] * _sigmoid(p_ref[prow, D_CONV:2 * D_CONV])
            for gi in range(CONV_GROUPS):
                gs = slice(gi * LANES, (gi + 1) * LANES)
                if c == 0:
                    uw = jnp.concatenate([hu_ref[:, gs], p_ref[0:CHUNK, gs]], axis=0)
                else:
                    uw = p_ref[r0 - HIST:r0 + CHUNK, gs]
                acc = None
                for r in range(SUBLANES):
                    part = None
                    for a in range((CONV_WIDTH - 1 - r) // SUBLANES + 1):
                        j = CONV_WIDTH - 1 - (SUBLANES * a + r)
                        lo = HIST - SUBLANES * (a + 1)
                        term = cw_ref[j:j + 1, gs] * uw[lo:lo + CHUNK + SUBLANES]
                        part = term if part is None else part + term
                    if r:
                        part = pltpu.roll(part, r, axis=0)
                    acc = part if acc is None else acc + part
                ug = acc[SUBLANES:] + cb_ref[:, gs]
                mu = jnp.mean(ug, axis=-1, keepdims=True)
                dv = ug - mu
                var = jnp.mean(dv * dv, axis=-1, keepdims=True)
                yn = dv * lax.rsqrt(var + EPS) * lng_ref[:, gs] + lnb_ref[:, gs]
                y_ref[prow, gs] = _silu(yn).astype(BF16)

            if c == 0:
                xw = jnp.concatenate([hx_ref[HIST - SUBLANES:HIST, :], p_ref[0:CHUNK, XB:XB + D_LRU]], axis=0)
            else:
                xw = p_ref[r0 - SUBLANES:r0 + CHUNK, XB:XB + D_LRU]
            u = lw_ref[LRU_CONV - 1:LRU_CONV, :] * xw
            for d in range(1, LRU_CONV):
                u = u + lw_ref[LRU_CONV - 1 - d:LRU_CONV - d, :] * pltpu.roll(xw, d, axis=0)
            u = u[SUBLANES:] + lcb_ref[...]
            gates = _dot(u.astype(BF16), wax_ref[...]) + bax
            r = _sigmoid(gates[:, 0:D_LRU])
            ig = _sigmoid(gates[:, D_LRU:2 * D_LRU])
            log_a = r * neg_c_softplus
            a = jnp.exp(log_a)
            mult = jnp.sqrt(jnp.tanh(-log_a) * (a * a + 1.0))
            first = (row0 + r0) == N_PAD
            bt = jnp.where(first, 1.0, mult) * (ig * u)
            a = jnp.where(first, 0.0, a)
            for sh in (1, 2, 4):
                keep = sub >= sh
                a_sh = jnp.where(keep, pltpu.roll(a, sh, axis=0), 1.0)
                b_sh = jnp.where(keep, pltpu.roll(bt, sh, axis=0), 0.0)
                bt = bt + a * b_sh
                a = a * a_sh
            hs = []
            for blk in range(CHUNK // SUBLANES):
                rs = slice(blk * SUBLANES, (blk + 1) * SUBLANES)
                hb = bt[rs, :] + a[rs, :] * carry
                carry = hb[SUBLANES - 1:SUBLANES, :]
                hs.append(hb)
            hseq = jnp.concatenate(hs, axis=0)
            dg = p_ref[prow, GC:GC + D_LRU]
            gelu = 0.5 * dg * (1.0 + jnp.tanh(0.7978845608028654 * (dg + 0.044715 * dg * dg * dg)))
            y_ref[prow, D_CONV:D_CONV + D_LRU] = (gelu * hseq).astype(BF16)

        hc_ref[0:1, :] = carry
        hu_ref[...] = p_ref[TL_MIX - HIST:TL_MIX, 0:D_CONV]
        hx_ref[...] = p_ref[TL_MIX - HIST:TL_MIX, XB:XB + D_LRU]
        out_ref[...] = h1_ref[...] + _dot(y_ref[...], wout_ref[...])

    _fused_steps(lambda: h_ref[...], (g1_ref, wg_ref, wu_ref, wd_ref), g_ref, win_ref, (h1a_ref, pa_ref),
                 (h1b_ref, pb_ref), acc_ref, un_ref, reset, mix)


def _odd_call(h, ffn, g, win, cw, cb, lng, lnb, lw, lcb, wax, bax, lam, wout):
    consts = ffn + (g, win, cw, cb, lng, lnb, lw, lcb, wax, bax, lam, wout)
    h_spec = pl.BlockSpec((TL_MIX, D_MODEL), lambda s: (jnp.minimum(s, N_TILES - 1), 0))
    scratch = [pltpu.VMEM((SUBLANES, D_LRU), F32),
               pltpu.VMEM((HIST, D_CONV), F32),
               pltpu.VMEM((HIST, D_LRU), F32)]
    return _fused_call(_odd_kernel, "ffn_odd_mixer", h, h_spec, consts, D_IN_ODD, scratch)


def _block_diag(w):
    g, n, _ = w.shape
    eye = jnp.eye(g, dtype=w.dtype)
    return jnp.einsum('gij,gh->gihj', w, eye).reshape(g * n, g * n)


def kernel(x, meta_tokens, ffn1_norm, ffn1_wg, ffn1_wu, ffn1_wd, mix_norm, ffn2_norm, ffn2_wg, ffn2_wu, ffn2_wd, w_in_even, pool_w, pool_scale, hgrn_lb_logits, hgrn_gnorm, w_out_even, w_in_odd, conv_w, conv_b, conv_ln_g, conv_ln_b, lru_conv_w, lru_conv_b, lru_wa, lru_ba, lru_wx, lru_bx, lru_lambda, w_out_odd, final_norm):
    bn = x.shape[0]
    x_flat = x.astype(F32).reshape(bn * SEQ, D_MODEL)
    head = jnp.concatenate([jnp.zeros((N_PAD, D_MODEL), F32), meta_tokens.astype(F32)], axis=0)

    row = lambda a: a.reshape(1, -1).astype(F32)
    bf = lambda a: a.astype(BF16)

    def ffn_args(norm, wg, wu, wd, l):
        return (row(norm[l]), bf(wg[l]), bf(wu[l]), bf(wd[l]))

    tril = jnp.tril(jnp.ones((CHUNK, CHUNK), BF16))

    h = _even_call(x_flat, head, ffn_args(ffn1_norm, ffn1_wg, ffn1_wu, ffn1_wd, 0),
                   row(mix_norm[0]), bf(w_in_even[0]), bf(_block_diag(pool_w[0])), row(pool_scale[0]),
                   hgrn_lb_logits.astype(F32), row(hgrn_gnorm[0]), bf(w_out_even[0]), tril)
    h = _ffn_call(h, ffn_args(ffn2_norm, ffn2_wg, ffn2_wu, ffn2_wd, 0))
    wax = jnp.concatenate([_block_diag(lru_wa[0]), _block_diag(lru_wx[0])], axis=1)
    bax = jnp.concatenate([lru_ba[0], lru_bx[0]]).reshape(1, -1).astype(F32)
    h = _odd_call(h, ffn_args(ffn1_norm, ffn1_wg, ffn1_wu, ffn1_wd, 1),
                  row(mix_norm[1]), bf(w_in_odd[0]), conv_w[0].astype(F32), row(conv_b[0]),
                  row(conv_ln_g[0]), row(conv_ln_b[0]), lru_conv_w[0].astype(F32), row(lru_conv_b[0]),
                  bf(wax), bax, row(lru_lambda[0]), bf(w_out_odd[0]))
    out = _ffn_final_call(h, ffn_args(ffn2_norm, ffn2_wg, ffn2_wu, ffn2_wd, 1), row(final_norm))
    return out.reshape(bn, SEQ, D_MODEL)
```

```python
import functools

import jax
import jax.numpy as jnp
from jax import lax
from jax.experimental import pallas as pl
from jax.experimental.pallas import tpu as pltpu

F32 = jnp.float32
BF16 = jnp.bfloat16

D_MODEL = 1024
BATCH = 8
SEQ = 4096
N_META = 16
D_FF = 2816
EPS = 1e-6
D_POOL = 256
HG_HEAD = 128
D_HGRN = 768
HG_HEADS = 6
CHUNK = 64
D_IN_EVEN = D_POOL + 4 * D_HGRN
D_CONV = 512
CONV_WIDTH = 31
CONV_GROUPS = 4
D_LRU = 512
LRU_CONV = 4
LRU_C = 8.0
D_IN_ODD = 2 * D_CONV + 2 * D_LRU
LANES = 128
SUBLANES = 8

HEAD = CHUNK
N_PAD = HEAD - N_META
L_PAD = HEAD + SEQ
ROWS = BATCH * L_PAD

TM_FFN = 320
TM_OUT = 512
FF_BOUNDS = (0, 1536, D_FF)
TL_MIX = 320
N_CHUNKS = TL_MIX // CHUNK
TILES_PER_SEQ = L_PAD // TL_MIX
N_TILES = ROWS // TL_MIX
HIST = 32
VMEM_LIMIT = 58 * 1024 * 1024


def _rms(x, g):
    ms = jnp.mean(x * x, axis=-1, keepdims=True)
    return x * lax.rsqrt(ms + EPS) * g


def _sigmoid(x):
    return 1.0 / (1.0 + jnp.exp(-x))


def _silu(x):
    return x * _sigmoid(x)


def _dot(a, b):
    return jnp.dot(a, b, preferred_element_type=F32)


def _dot_nt(a, b):
    return lax.dot_general(a, b, (((1,), (1,)), ((), ())), preferred_element_type=F32)


def _dot_tn(a, b):
    return lax.dot_general(a, b, (((0,), (0,)), ((), ())), preferred_element_type=F32)


def _resident():
    return pl.BlockSpec(memory_space=pltpu.VMEM)


def _swiglu(xn, wg_ref, wu_ref, wd_ref):
    acc = None
    for lo, hi in zip(FF_BOUNDS[:-1], FF_BOUNDS[1:]):
        gate = _dot(xn, wg_ref[:, lo:hi])
        up = _dot(xn, wu_ref[:, lo:hi])
        hid = (_silu(gate) * up).astype(BF16)
        part = _dot(hid, wd_ref[lo:hi, :])
        acc = part if acc is None else acc + part
    return acc


def _ffn_kernel(n_ffn, splice_head, final, *refs):
    out_ref = refs[-1]
    h = refs[0][...]
    nxt = 1
    if splice_head:
        first = pl.program_id(0) % TILES_PER_SEQ == 0
        spliced = jnp.concatenate([refs[1][...], h[0:TL_MIX - HEAD]], axis=0)
        h = jnp.where(first, spliced, h)
        nxt = 2
    for i in range(n_ffn):
        g_ref, wg_ref, wu_ref, wd_ref = refs[nxt + 4 * i: nxt + 4 + 4 * i]
        xn = _rms(h, g_ref[...]).astype(BF16)
        h = h + 0.5 * _swiglu(xn, wg_ref, wu_ref, wd_ref)
    if final:
        h = _rms(h, refs[nxt + 4 * n_ffn][...])
    out_ref[...] = h


def _ffn_pallas(name, kernel_fn, grid, in_spec, out_spec, out_rows, args):
    return pl.pallas_call(
        kernel_fn,
        out_shape=jax.ShapeDtypeStruct((out_rows, D_MODEL), F32),
        grid=(grid,),
        in_specs=[in_spec] + [_resident()] * (len(args) - 1),
        out_specs=out_spec,
        compiler_params=pltpu.CompilerParams(
            dimension_semantics=("parallel",), vmem_limit_bytes=VMEM_LIMIT),
        name=name,
    )(*args)


def _x_tile_offset(n):
    return CHUNK * ((n // TILES_PER_SEQ) * (SEQ // CHUNK) + jnp.maximum((n % TILES_PER_SEQ) * N_CHUNKS - 1, 0))


def _ffn_first_call(x_flat, head, ffn):
    in_spec = pl.BlockSpec((pl.Element(TL_MIX), pl.Element(D_MODEL)), lambda n: (_x_tile_offset(n), 0))
    out_spec = pl.BlockSpec((TL_MIX, D_MODEL), lambda n: (n, 0))
    return _ffn_pallas("ffn_first", functools.partial(_ffn_kernel, 1, True, False), N_TILES, in_spec, out_spec,
                       ROWS, (x_flat, head) + ffn)


def _ffn_mid_call(h, ffn_a, ffn_b):
    spec = pl.BlockSpec((TM_FFN, D_MODEL), lambda i: (i, 0))
    return _ffn_pallas("ffn_mid", functools.partial(_ffn_kernel, 2, False, False), ROWS // TM_FFN, spec, spec,
                       ROWS, (h,) + ffn_a + ffn_b)


def _ffn_final_call(h, ffn, final_g):
    per_seq = SEQ // TM_OUT
    in_spec = pl.BlockSpec((pl.Element(TM_OUT), pl.Element(D_MODEL)),
                           lambda i: (CHUNK * ((i // per_seq) * (L_PAD // CHUNK) + HEAD // CHUNK
                                               + (i % per_seq) * (TM_OUT // CHUNK)), 0))
    out_spec = pl.BlockSpec((TM_OUT, D_MODEL), lambda i: (i, 0))
    return _ffn_pallas("ffn_final", functools.partial(_ffn_kernel, 1, False, True), BATCH * per_seq, in_spec,
                       out_spec, BATCH * SEQ, (h,) + ffn + (final_g,))


def _mixer_steps(h_ref, hn_ref, g_ref, win_ref, pa_ref, pb_ref, reset, mix):
    s = pl.program_id(0)
    g = g_ref[...]
    d_in = win_ref.shape[1]
    n_col_tiles = d_in // (2 * LANES)
    bounds = [2 * LANES * ((n_col_tiles * i) // N_CHUNKS) for i in range(N_CHUNKS + 1)]

    @pl.when(s == 0)
    def _():
        pa_ref[...] = _dot(_rms(h_ref[...], g).astype(BF16), win_ref[...])

    @pl.when(s % TILES_PER_SEQ == 0)
    def _():
        reset()

    def step(cur_ref, nxt_ref):
        un = _rms(hn_ref[...], g).astype(BF16)

        def project_piece(c):
            lo, hi = bounds[c], bounds[c + 1]
            if hi > lo:
                nxt_ref[:, lo:hi] = _dot(un, win_ref[:, lo:hi])

        mix(cur_ref, project_piece)

    @pl.when(s % 2 == 0)
    def _():
        step(pa_ref, pb_ref)

    @pl.when(s % 2 == 1)
    def _():
        step(pb_ref, pa_ref)


def _mixer_call(body, name, h, consts, d_in, scratch):
    cur_spec = pl.BlockSpec((TL_MIX, D_MODEL), lambda s: (s, 0))
    nxt_spec = pl.BlockSpec((TL_MIX, D_MODEL), lambda s: (jnp.minimum(s + 1, N_TILES - 1), 0))
    return pl.pallas_call(
        body,
        out_shape=jax.ShapeDtypeStruct((ROWS, D_MODEL), F32),
        grid=(N_TILES,),
        in_specs=[cur_spec, nxt_spec] + [_resident()] * len(consts),
        out_specs=cur_spec,
        scratch_shapes=[pltpu.VMEM((TL_MIX, d_in), F32),
                        pltpu.VMEM((TL_MIX, d_in), F32),
                        pltpu.VMEM((TL_MIX, D_MODEL), BF16),
                        ] + scratch,
        compiler_params=pltpu.CompilerParams(
            dimension_semantics=("arbitrary",), vmem_limit_bytes=VMEM_LIMIT),
        name=name,
    )(h, h, *consts)


def _tile_row0():
    return (pl.program_id(0) % TILES_PER_SEQ) * TL_MIX


def _even_kernel(h_ref, hn_ref, g_ref, win_ref, poolw_ref, pscale_ref, lbl_ref, gn_ref, wout_ref, tril_ref,
                 out_ref, pa_ref, pb_ref, y_ref, st_ref, hist_ref):
    def reset():
        st_ref[...] = jnp.zeros_like(st_ref)
        hist_ref[...] = jnp.zeros_like(hist_ref)

    def mix(p_ref, project_piece):
        lbl = lbl_ref[...]
        mx = jnp.maximum(lbl[0:1, :], lbl[1:2, :])
        e0 = jnp.exp(lbl[0:1, :] - mx)
        e1 = jnp.exp(lbl[1:2, :] - mx)
        lb = e0 / (e0 + e1)
        log_lb = jnp.log(lb)
        log_1mlb = jnp.log1p(-lb)
        one_mlb = 1.0 - lb
        gn = gn_ref[...]
        tril = tril_ref[...]
        causal = tril > 0.5
        pscale = pscale_ref[...]
        lane = lax.broadcasted_iota(jnp.int32, (CHUNK, D_POOL), 1)
        win = jnp.where(lane < 64, 2.0, jnp.where(lane < 128, 4.0, jnp.where(lane < 192, 8.0, 16.0)))
        pos = lax.broadcasted_iota(jnp.int32, (CHUNK, D_POOL), 0) + (_tile_row0() + 1 - N_PAD)

        for c in range(N_CHUNKS):
            project_piece(c)
            r0 = c * CHUNK
            prow = slice(r0, r0 + CHUNK)

            if c == 0:
                w = jnp.concatenate([hist_ref[HIST - 16:HIST, :], p_ref[0:CHUNK, 0:D_POOL]], axis=0)
            else:
                w = p_ref[r0 - 16:r0 + CHUNK, 0:D_POOL]
            s2 = w + pltpu.roll(w, 1, axis=0)
            s4 = s2 + pltpu.roll(s2, 2, axis=0)
            s8 = s4 + pltpu.roll(s4, 4, axis=0)
            s16 = s8 + pltpu.roll(s8, 8, axis=0)
            x = w[16:]
            ssel = jnp.where(lane < 64, s2[16:],
                             jnp.where(lane < 128, s4[16:], jnp.where(lane < 192, s8[16:], s16[16:])))
            cnt = jnp.minimum(jnp.maximum(pos + r0, 1).astype(F32), win)
            mixed = ssel / cnt - x
            ya = _dot(mixed.astype(BF16), poolw_ref[...]) * pscale
            y_ref[prow, 0:D_POOL] = ya.astype(BF16)

            q = _silu(p_ref[prow, D_POOL:D_POOL + D_HGRN])
            z = p_ref[prow, D_POOL + D_HGRN:D_POOL + 2 * D_HGRN]
            v = p_ref[prow, D_POOL + 2 * D_HGRN:D_POOL + 3 * D_HGRN].astype(BF16)
            ez = jnp.exp(-jnp.abs(z))
            wz = 1.0 + ez
            log_sig = jnp.minimum(z, 0.0) - jnp.log(wz)
            rz = 1.0 / wz
            k = one_mlb * jnp.where(z >= 0, ez * rz, rz)
            bb = log_1mlb + log_sig
            log_f = jnp.maximum(log_lb, bb) + jnp.log(1.0 + jnp.exp(-jnp.abs(log_lb - bb)))
            lf_hi = log_f.astype(BF16)
            lf_lo = (log_f - lf_hi.astype(F32)).astype(BF16)
            b = _dot(tril, lf_hi) + _dot(tril, lf_lo)
            b_end = b[CHUNK - 1:CHUNK, :]
            mid = 0.5 * b_end
            e_mid = jnp.exp(mid)
            bm = b - mid
            qt = q * jnp.exp(bm)
            kt = k * jnp.exp(-bm)
            qi = (qt * e_mid).astype(BF16)
            ks = (kt * e_mid).astype(BF16)
            qt = qt.astype(BF16)
            kt = kt.astype(BF16)
            dec = jnp.exp(b_end)
            for hd in range(HG_HEADS):
                hs = slice(hd * HG_HEAD, (hd + 1) * HG_HEAD)
                a = jnp.where(causal, _dot_nt(qt[:, hs], kt[:, hs]), 0.0)
                st = st_ref[hd]
                o = _dot(a.astype(BF16), v[:, hs]) + _dot_nt(qi[:, hs], st.astype(BF16))
                st_ref[hd] = st * dec[:, hs] + _dot_tn(v[:, hs], ks[:, hs])
                o = o * lax.rsqrt(jnp.mean(o * o, axis=-1, keepdims=True) + EPS) * gn
                gcol = D_POOL + 3 * D_HGRN + hd * HG_HEAD
                gg = p_ref[prow, gcol:gcol + HG_HEAD]
                y_ref[prow, D_POOL + hd * HG_HEAD:D_POOL + (hd + 1) * HG_HEAD] = (o * _silu(gg)).astype(BF16)

        hist_ref[...] = p_ref[TL_MIX - HIST:TL_MIX, 0:D_POOL]
        out_ref[...] = h_ref[...] + _dot(y_ref[...], wout_ref[...])

    _mixer_steps(h_ref, hn_ref, g_ref, win_ref, pa_ref, pb_ref, reset, mix)


def _even_call(h, g, win, poolw, pscale, lbl, gn, wout, tril):
    consts = (g, win, poolw, pscale, lbl, gn, wout, tril)
    scratch = [pltpu.VMEM((HG_HEADS, HG_HEAD, HG_HEAD), F32),
               pltpu.VMEM((HIST, D_POOL), F32)]
    return _mixer_call(_even_kernel, "even_mixer", h, consts, D_IN_EVEN, scratch)


def _odd_kernel(h_ref, hn_ref, g_ref, win_ref, cw_ref, cb_ref, lng_ref, lnb_ref, lw_ref, lcb_ref, wax_ref,
                bax_ref, lam_ref, wout_ref, out_ref, pa_ref, pb_ref, y_ref, hc_ref, hu_ref, hx_ref):
    XB = 2 * D_CONV
    GC = 2 * D_CONV + D_LRU

    def reset():
        hc_ref[...] = jnp.zeros_like(hc_ref)
        hu_ref[...] = jnp.zeros_like(hu_ref)
        hx_ref[...] = jnp.zeros_like(hx_ref)

    def mix(p_ref, project_piece):
        lam = lam_ref[...]
        neg_c_softplus = -LRU_C * (jnp.maximum(-lam, 0.0) + jnp.log1p(jnp.exp(-jnp.abs(lam))))
        bax = bax_ref[...]
        sub = lax.broadcasted_iota(jnp.int32, (CHUNK, D_LRU), 0) & (SUBLANES - 1)
        row0 = lax.broadcasted_iota(jnp.int32, (CHUNK, D_LRU), 0) + _tile_row0()
        carry = hc_ref[0:1, :]

        for c in range(N_CHUNKS):
            project_piece(c)
            r0 = c * CHUNK
            prow = slice(r0, r0 + CHUNK)

            p_ref[prow, 0:D_CONV] = p_ref[prow, 0:D_CONV] * _sigmoid(p_ref[prow, D_CONV:2 * D_CONV])
            for gi in range(CONV_GROUPS):
                gs = slice(gi * LANES, (gi + 1) * LANES)
                if c == 0:
                    uw = jnp.concatenate([hu_ref[:, gs], p_ref[0:CHUNK, gs]], axis=0)
                else:
                    uw = p_ref[r0 - HIST:r0 + CHUNK, gs]
                acc = None
                for r in range(SUBLANES):
                    part = None
                    for a in range((CONV_WIDTH - 1 - r) // SUBLANES + 1):
                        j = CONV_WIDTH - 1 - (SUBLANES * a + r)
                        lo = HIST - SUBLANES * (a + 1)
                        term = cw_ref[j:j + 1, gs] * uw[lo:lo + CHUNK + SUBLANES]
                        part = term if part is None else part + term
                    if r:
                        part = pltpu.roll(part, r, axis=0)
                    acc = part if acc is None else acc + part
                ug = acc[SUBLANES:] + cb_ref[:, gs]
                mu = jnp.mean(ug, axis=-1, keepdims=True)
                dv = ug - mu
                var = jnp.mean(dv * dv, axis=-1, keepdims=True)
                yn = dv * lax.rsqrt(var + EPS) * lng_ref[:, gs] + lnb_ref[:, gs]
                y_ref[prow, gs] = _silu(yn).astype(BF16)

            if c == 0:
                xw = jnp.concatenate([hx_ref[HIST - SUBLANES:HIST, :], p_ref[0:CHUNK, XB:XB + D_LRU]], axis=0)
            else:
                xw = p_ref[r0 - SUBLANES:r0 + CHUNK, XB:XB + D_LRU]
            u = lw_ref[LRU_CONV - 1:LRU_CONV, :] * xw
            for d in range(1, LRU_CONV):
                u = u + lw_ref[LRU_CONV - 1 - d:LRU_CONV - d, :] * pltpu.roll(xw, d, axis=0)
            u = u[SUBLANES:] + lcb_ref[...]
            gates = _dot(u.astype(BF16), wax_ref[...]) + bax
            r = _sigmoid(gates[:, 0:D_LRU])
            ig = _sigmoid(gates[:, D_LRU:2 * D_LRU])
            log_a = r * neg_c_softplus
            a = jnp.exp(log_a)
            mult = jnp.sqrt(jnp.tanh(-log_a) * (a * a + 1.0))
            first = (row0 + r0) == N_PAD
            bt = jnp.where(first, 1.0, mult) * (ig * u)
            a = jnp.where(first, 0.0, a)
            for sh in (1, 2, 4):
                keep = sub >= sh
                a_sh = jnp.where(keep, pltpu.roll(a, sh, axis=0), 1.0)
                b_sh = jnp.where(keep, pltpu.roll(bt, sh, axis=0), 0.0)
                bt = bt + a * b_sh
                a = a * a_sh
            hs = []
            for blk in range(CHUNK // SUBLANES):
                rs = slice(blk * SUBLANES, (blk + 1) * SUBLANES)
                hb = bt[rs, :] + a[rs, :] * carry
                carry = hb[SUBLANES - 1:SUBLANES, :]
                hs.append(hb)
            hseq = jnp.concatenate(hs, axis=0)
            dg = p_ref[prow, GC:GC + D_LRU]
            gelu = 0.5 * dg * (1.0 + jnp.tanh(0.7978845608028654 * (dg + 0.044715 * dg * dg * dg)))
            y_ref[prow, D_CONV:D_CONV + D_LRU] = (gelu * hseq).astype(BF16)

        hc_ref[0:1, :] = carry
        hu_ref[...] = p_ref[TL_MIX - HIST:TL_MIX, 0:D_CONV]
        hx_ref[...] = p_ref[TL_MIX - HIST:TL_MIX, XB:XB + D_LRU]
        out_ref[...] = h_ref[...] + _dot(y_ref[...], wout_ref[...])

    _mixer_steps(h_ref, hn_ref, g_ref, win_ref, pa_ref, pb_ref, reset, mix)


def _odd_call(h, g, win, cw, cb, lng, lnb, lw, lcb, wax, bax, lam, wout):
    consts = (g, win, cw, cb, lng, lnb, lw, lcb, wax, bax, lam, wout)
    scratch = [pltpu.VMEM((SUBLANES, D_LRU), F32),
               pltpu.VMEM((HIST, D_CONV), F32),
               pltpu.VMEM((HIST, D_LRU), F32)]
    return _mixer_call(_odd_kernel, "odd_mixer", h, consts, D_IN_ODD, scratch)


def _block_diag(w):
    g, n, _ = w.shape
    eye = jnp.eye(g, dtype=w.dtype)
    return jnp.einsum('gij,gh->gihj', w, eye).reshape(g * n, g * n)


def kernel(x, meta_tokens, ffn1_norm, ffn1_wg, ffn1_wu, ffn1_wd, mix_norm, ffn2_norm, ffn2_wg, ffn2_wu, ffn2_wd, w_in_even, pool_w, pool_scale, hgrn_lb_logits, hgrn_gnorm, w_out_even, w_in_odd, conv_w, conv_b, conv_ln_g, conv_ln_b, lru_conv_w, lru_conv_b, lru_wa, lru_ba, lru_wx, lru_bx, lru_lambda, w_out_odd, final_norm):
    bn = x.shape[0]
    x_flat = x.astype(F32).reshape(bn * SEQ, D_MODEL)
    head = jnp.concatenate([jnp.zeros((N_PAD, D_MODEL), F32), meta_tokens.astype(F32)], axis=0)

    row = lambda a: a.reshape(1, -1).astype(F32)
    bf = lambda a: a.astype(BF16)

    def ffn_args(norm, wg, wu, wd, l):
        return (row(norm[l]), bf(wg[l]), bf(wu[l]), bf(wd[l]))

    tril = jnp.tril(jnp.ones((CHUNK, CHUNK), BF16))

    h = _ffn_first_call(x_flat, head, ffn_args(ffn1_norm, ffn1_wg, ffn1_wu, ffn1_wd, 0))
    h = _even_call(h, row(mix_norm[0]), bf(w_in_even[0]), bf(_block_diag(pool_w[0])), row(pool_scale[0]),
                   hgrn_lb_logits.astype(F32), row(hgrn_gnorm[0]), bf(w_out_even[0]), tril)
    h = _ffn_mid_call(h, ffn_args(ffn2_norm, ffn2_wg, ffn2_wu, ffn2_wd, 0),
                      ffn_args(ffn1_norm, ffn1_wg, ffn1_wu, ffn1_wd, 1))
    wax = jnp.concatenate([_block_diag(lru_wa[0]), _block_diag(lru_wx[0])], axis=1)
    bax = jnp.concatenate([lru_ba[0], lru_bx[0]]).reshape(1, -1).astype(F32)
    h = _odd_call(h, row(mix_norm[1]), bf(w_in_odd[0]), conv_w[0].astype(F32), row(conv_b[0]),
                  row(conv_ln_g[0]), row(conv_ln_b[0]), lru_conv_w[0].astype(F32), row(lru_conv_b[0]),
                  bf(wax), bax, row(lru_lambda[0]), bf(w_out_odd[0]))
    out = _ffn_final_call(h, ffn_args(ffn2_norm, ffn2_wg, ffn2_wu, ffn2_wd, 1), row(final_norm))
    return out.reshape(bn, SEQ, D_MODEL)
```

```python
import functools

import jax
import jax.numpy as jnp
from jax import lax
from jax.experimental import pallas as pl
from jax.experimental.pallas import tpu as pltpu

F32 = jnp.float32
BF16 = jnp.bfloat16

D_MODEL = 1024
BATCH = 8
SEQ = 4096
N_META = 16
D_FF = 2816
EPS = 1e-6
D_POOL = 256
HG_HEAD = 128
D_HGRN = 768
HG_HEADS = 6
CHUNK = 64
D_IN_EVEN = D_POOL + 4 * D_HGRN
D_CONV = 512
CONV_WIDTH = 31
CONV_GROUPS = 4
D_LRU = 512
LRU_CONV = 4
LRU_C = 8.0
D_IN_ODD = 2 * D_CONV + 2 * D_LRU
LANES = 128
SUBLANES = 8

HEAD = CHUNK
N_PAD = HEAD - N_META
L_PAD = HEAD + SEQ
ROWS = BATCH * L_PAD

TM_FFN = 640
TM_OUT = 1024
FFN_SUBTILES = 2
FF_BOUNDS = (0, 1536, D_FF)
TL_MIX = 320
N_CHUNKS = TL_MIX // CHUNK
TILES_PER_SEQ = L_PAD // TL_MIX
N_TILES = ROWS // TL_MIX
HIST = 32
VMEM_LIMIT = 58 * 1024 * 1024


def _rms(x, g):
    ms = jnp.mean(x * x, axis=-1, keepdims=True)
    return x * lax.rsqrt(ms + EPS) * g


def _sigmoid(x):
    return 1.0 / (1.0 + jnp.exp(-x))


def _silu(x):
    return x * _sigmoid(x)


def _dot(a, b):
    return jnp.dot(a, b, preferred_element_type=F32)


def _dot_nt(a, b):
    return lax.dot_general(a, b, (((1,), (1,)), ((), ())), preferred_element_type=F32)


def _dot_tn(a, b):
    return lax.dot_general(a, b, (((0,), (0,)), ((), ())), preferred_element_type=F32)


def _resident():
    return pl.BlockSpec(memory_space=pltpu.VMEM)


def _swiglu(xn, wg_ref, wu_ref, wd_ref):
    acc = None
    for lo, hi in zip(FF_BOUNDS[:-1], FF_BOUNDS[1:]):
        gate = _dot(xn, wg_ref[:, lo:hi])
        up = _dot(xn, wu_ref[:, lo:hi])
        hid = (_silu(gate) * up).astype(BF16)
        part = _dot(hid, wd_ref[lo:hi, :])
        acc = part if acc is None else acc + part
    return acc


def _ffn_kernel(splice_head, final, n_sub, *refs):
    out_ref = refs[-1]
    nxt = 2 if splice_head else 1
    g_ref, wg_ref, wu_ref, wd_ref = refs[nxt:nxt + 4]
    sub = refs[0].shape[0] // n_sub
    for r in range(n_sub):
        rows = slice(r * sub, (r + 1) * sub)
        h = refs[0][rows, :]
        if splice_head:
            first = pl.program_id(0) % TILES_PER_SEQ == 0
            spliced = jnp.concatenate([refs[1][...], h[0:TL_MIX - HEAD]], axis=0)
            h = jnp.where(first, spliced, h)
        xn = _rms(h, g_ref[...]).astype(BF16)
        h = h + 0.5 * _swiglu(xn, wg_ref, wu_ref, wd_ref)
        if final:
            h = _rms(h, refs[nxt + 4][...])
        out_ref[rows, :] = h


def _ffn_pallas(name, kernel_fn, grid, in_spec, out_spec, out_rows, args):
    return pl.pallas_call(
        kernel_fn,
        out_shape=jax.ShapeDtypeStruct((out_rows, D_MODEL), F32),
        grid=(grid,),
        in_specs=[in_spec] + [_resident()] * (len(args) - 1),
        out_specs=out_spec,
        compiler_params=pltpu.CompilerParams(
            dimension_semantics=("parallel",), vmem_limit_bytes=VMEM_LIMIT),
        name=name,
    )(*args)


def _x_tile_offset(n):
    return CHUNK * ((n // TILES_PER_SEQ) * (SEQ // CHUNK) + jnp.maximum((n % TILES_PER_SEQ) * N_CHUNKS - 1, 0))


def _ffn_first_call(x_flat, head, ffn):
    in_spec = pl.BlockSpec((pl.Element(TL_MIX), pl.Element(D_MODEL)), lambda n: (_x_tile_offset(n), 0))
    out_spec = pl.BlockSpec((TL_MIX, D_MODEL), lambda n: (n, 0))
    return _ffn_pallas("ffn_first", functools.partial(_ffn_kernel, True, False, 1), N_TILES, in_spec, out_spec,
                       ROWS, (x_flat, head) + ffn)


def _ffn_mid_call(h, ffn):
    spec = pl.BlockSpec((TM_FFN, D_MODEL), lambda i: (i, 0))
    return _ffn_pallas("ffn_mid", functools.partial(_ffn_kernel, False, False, FFN_SUBTILES), ROWS // TM_FFN,
                       spec, spec, ROWS, (h,) + ffn)


def _ffn_final_call(h, ffn, final_g):
    per_seq = SEQ // TM_OUT
    in_spec = pl.BlockSpec((pl.Element(TM_OUT), pl.Element(D_MODEL)),
                           lambda i: (CHUNK * ((i // per_seq) * (L_PAD // CHUNK) + HEAD // CHUNK
                                               + (i % per_seq) * (TM_OUT // CHUNK)), 0))
    out_spec = pl.BlockSpec((TM_OUT, D_MODEL), lambda i: (i, 0))
    return _ffn_pallas("ffn_final", functools.partial(_ffn_kernel, False, True, FFN_SUBTILES), BATCH * per_seq,
                       in_spec, out_spec, BATCH * SEQ, (h,) + ffn + (final_g,))


def _mixer_steps(h_ref, hn_ref, g_ref, win_ref, pa_ref, pb_ref, reset, mix):
    s = pl.program_id(0)
    g = g_ref[...]
    d_in = win_ref.shape[1]
    n_col_tiles = d_in // (2 * LANES)
    bounds = [2 * LANES * ((n_col_tiles * i) // N_CHUNKS) for i in range(N_CHUNKS + 1)]

    @pl.when(s == 0)
    def _():
        pa_ref[...] = _dot(_rms(h_ref[...], g).astype(BF16), win_ref[...])

    @pl.when(s % TILES_PER_SEQ == 0)
    def _():
        reset()

    def step(cur_ref, nxt_ref):
        un = _rms(hn_ref[...], g).astype(BF16)

        def project_piece(c):
            lo, hi = bounds[c], bounds[c + 1]
            if hi > lo:
                nxt_ref[:, lo:hi] = _dot(un, win_ref[:, lo:hi])

        mix(cur_ref, project_piece)

    @pl.when(s % 2 == 0)
    def _():
        step(pa_ref, pb_ref)

    @pl.when(s % 2 == 1)
    def _():
        step(pb_ref, pa_ref)


def _mixer_call(body, name, h, consts, d_in, scratch):
    cur_spec = pl.BlockSpec((TL_MIX, D_MODEL), lambda s: (s, 0))
    nxt_spec = pl.BlockSpec((TL_MIX, D_MODEL), lambda s: (jnp.minimum(s + 1, N_TILES - 1), 0))
    return pl.pallas_call(
        body,
        out_shape=jax.ShapeDtypeStruct((ROWS, D_MODEL), F32),
        grid=(N_TILES,),
        in_specs=[cur_spec, nxt_spec] + [_resident()] * len(consts),
        out_specs=cur_spec,
        scratch_shapes=[pltpu.VMEM((TL_MIX, d_in), F32),
                        pltpu.VMEM((TL_MIX, d_in), F32),
                        pltpu.VMEM((TL_MIX, D_MODEL), BF16),
                        ] + scratch,
        compiler_params=pltpu.CompilerParams(
            dimension_semantics=("arbitrary",), vmem_limit_bytes=VMEM_LIMIT),
        name=name,
    )(h, h, *consts)


def _tile_row0():
    return (pl.program_id(0) % TILES_PER_SEQ) * TL_MIX


DECAY_SAFE = 150.0


def _even_kernel(h_ref, hn_ref, g_ref, win_ref, poolw_ref, pscale_ref, lbl_ref, gn_ref, wout_ref, tril_ref,
                 out_ref, pa_ref, pb_ref, y_ref, st_ref, hist_ref, qt_ref, kt_ref, qi_ref, ks_ref, dec_ref, a_ref,
                 sprev_ref, bs_ref, kk_ref):
    QC = D_POOL
    FC = D_POOL + D_HGRN
    VC = D_POOL + 2 * D_HGRN
    GC = D_POOL + 3 * D_HGRN

    def reset():
        st_ref[...] = jnp.zeros_like(st_ref)
        hist_ref[...] = jnp.zeros_like(hist_ref)

    def mix(p_ref, project_piece):
        lbl = lbl_ref[...]
        mx = jnp.maximum(lbl[0:1, :], lbl[1:2, :])
        e0 = jnp.exp(lbl[0:1, :] - mx)
        e1 = jnp.exp(lbl[1:2, :] - mx)
        lb = e0 / (e0 + e1)
        log_lb = jnp.log(lb)
        log_1mlb = jnp.log1p(-lb)
        one_mlb = 1.0 - lb
        gn = gn_ref[...]
        tril = tril_ref[...]
        causal = tril > 0.5
        pscale = pscale_ref[...]
        lane = lax.broadcasted_iota(jnp.int32, (CHUNK, D_POOL), 1)
        win = jnp.where(lane < 64, 2.0, jnp.where(lane < 128, 4.0, jnp.where(lane < 192, 8.0, 16.0)))
        pos = lax.broadcasted_iota(jnp.int32, (CHUNK, D_POOL), 0) + (_tile_row0() + 1 - N_PAD)

        def gates(prow):
            q = _silu(p_ref[prow, QC:QC + D_HGRN])
            z = p_ref[prow, FC:FC + D_HGRN]
            ez = jnp.exp(-jnp.abs(z))
            wz = 1.0 + ez
            log_sig = jnp.minimum(z, 0.0) - jnp.log(wz)
            rz = 1.0 / wz
            k = one_mlb * jnp.where(z >= 0, ez * rz, rz)
            bb = log_1mlb + log_sig
            log_f = jnp.maximum(log_lb, bb) + jnp.log(1.0 + jnp.exp(-jnp.abs(log_lb - bb)))
            lf_hi = log_f.astype(BF16)
            lf_lo = (log_f - lf_hi.astype(F32)).astype(BF16)
            b = _dot(tril, lf_hi) + _dot(tril, lf_lo)
            return q, k, b

        worst = jnp.zeros((1, D_HGRN), F32)
        for c in range(N_CHUNKS):
            project_piece(c)
            r0 = c * CHUNK
            prow = slice(r0, r0 + CHUNK)

            if c == 0:
                w = jnp.concatenate([hist_ref[HIST - 16:HIST, :], p_ref[0:CHUNK, 0:D_POOL]], axis=0)
            else:
                w = p_ref[r0 - 16:r0 + CHUNK, 0:D_POOL]
            s2 = w + pltpu.roll(w, 1, axis=0)
            s4 = s2 + pltpu.roll(s2, 2, axis=0)
            s8 = s4 + pltpu.roll(s4, 4, axis=0)
            s16 = s8 + pltpu.roll(s8, 8, axis=0)
            x = w[16:]
            ssel = jnp.where(lane < 64, s2[16:],
                             jnp.where(lane < 128, s4[16:], jnp.where(lane < 192, s8[16:], s16[16:])))
            cnt = jnp.minimum(jnp.maximum(pos + r0, 1).astype(F32), win)
            mixed = ssel / cnt - x
            ya = _dot(mixed.astype(BF16), poolw_ref[...]) * pscale
            y_ref[prow, 0:D_POOL] = ya.astype(BF16)

            q, k, b = gates(prow)
            b_end = b[CHUNK - 1:CHUNK, :]
            bm = b - 0.5 * b_end
            qt_ref[prow, :] = (q * jnp.exp(bm)).astype(BF16)
            kt_ref[prow, :] = (k * jnp.exp(-bm)).astype(BF16)
            qi_ref[prow, :] = (q * jnp.exp(b)).astype(BF16)
            ks_ref[prow, :] = (k * jnp.exp(b_end - b)).astype(BF16)
            dec_ref[c:c + 1, :] = jnp.exp(b_end)
            worst = jnp.maximum(worst, -b_end)

        for c in range(N_CHUNKS):
            prow = slice(c * CHUNK, (c + 1) * CHUNK)
            for hd in range(HG_HEADS):
                hs = slice(hd * HG_HEAD, (hd + 1) * HG_HEAD)
                a = jnp.where(causal, _dot_nt(qt_ref[prow, hs], kt_ref[prow, hs]), 0.0)
                a_ref[hd, prow, 0:CHUNK] = a.astype(BF16)
                v = p_ref[prow, VC + hd * HG_HEAD:VC + (hd + 1) * HG_HEAD].astype(BF16)
                st = st_ref[hd]
                sprev_ref[c * HG_HEADS + hd] = st.astype(BF16)
                st_ref[hd] = st * dec_ref[c:c + 1, hs] + _dot_tn(v, ks_ref[prow, hs])

        @pl.when(jnp.max(worst) > DECAY_SAFE)
        def _():
            col = lax.broadcasted_iota(jnp.int32, (CHUNK, CHUNK), 1)
            for c in range(N_CHUNKS):
                prow = slice(c * CHUNK, (c + 1) * CHUNK)
                q, k, b = gates(prow)
                bs_ref[...] = b
                kk_ref[...] = k

                def column(s, accs):
                    wgt = q * jnp.exp(jnp.minimum(b - bs_ref[pl.ds(s, 1), :], 0.0)) * kk_ref[pl.ds(s, 1), :]
                    out = []
                    for hd in range(HG_HEADS):
                        d = jnp.sum(wgt[:, hd * HG_HEAD:(hd + 1) * HG_HEAD], axis=-1, keepdims=True)
                        out.append(jnp.where(col == s, d, accs[hd]))
                    return tuple(out)

                accs = lax.fori_loop(0, CHUNK, column, tuple(jnp.zeros((CHUNK, CHUNK), F32) for _ in range(HG_HEADS)))
                for hd in range(HG_HEADS):
                    a_ref[hd, prow, 0:CHUNK] = jnp.where(causal, accs[hd], 0.0).astype(BF16)

        for c in range(N_CHUNKS):
            prow = slice(c * CHUNK, (c + 1) * CHUNK)
            for hd in range(HG_HEADS):
                hs = slice(hd * HG_HEAD, (hd + 1) * HG_HEAD)
                v = p_ref[prow, VC + hd * HG_HEAD:VC + (hd + 1) * HG_HEAD].astype(BF16)
                o = _dot(a_ref[hd, prow, 0:CHUNK], v) + _dot_nt(qi_ref[prow, hs], sprev_ref[c * HG_HEADS + hd])
                o = o * lax.rsqrt(jnp.mean(o * o, axis=-1, keepdims=True) + EPS) * gn
                gg = p_ref[prow, GC + hd * HG_HEAD:GC + (hd + 1) * HG_HEAD]
                y_ref[prow, D_POOL + hd * HG_HEAD:D_POOL + (hd + 1) * HG_HEAD] = (o * _silu(gg)).astype(BF16)

        hist_ref[...] = p_ref[TL_MIX - HIST:TL_MIX, 0:D_POOL]
        out_ref[...] = h_ref[...] + _dot(y_ref[...], wout_ref[...])

    _mixer_steps(h_ref, hn_ref, g_ref, win_ref, pa_ref, pb_ref, reset, mix)


def _even_call(h, g, win, poolw, pscale, lbl, gn, wout, tril):
    consts = (g, win, poolw, pscale, lbl, gn, wout, tril)
    scratch = [pltpu.VMEM((HG_HEADS, HG_HEAD, HG_HEAD), F32),
               pltpu.VMEM((HIST, D_POOL), F32),
               pltpu.VMEM((TL_MIX, D_HGRN), BF16),
               pltpu.VMEM((TL_MIX, D_HGRN), BF16),
               pltpu.VMEM((TL_MIX, D_HGRN), BF16),
               pltpu.VMEM((TL_MIX, D_HGRN), BF16),
               pltpu.VMEM((SUBLANES, D_HGRN), F32),
               pltpu.VMEM((HG_HEADS, TL_MIX, LANES), BF16),
               pltpu.VMEM((N_CHUNKS * HG_HEADS, HG_HEAD, HG_HEAD), BF16),
               pltpu.VMEM((CHUNK, D_HGRN), F32),
               pltpu.VMEM((CHUNK, D_HGRN), F32)]
    return _mixer_call(_even_kernel, "even_mixer", h, consts, D_IN_EVEN, scratch)


def _odd_kernel(h_ref, hn_ref, g_ref, win_ref, cw_ref, cb_ref, lng_ref, lnb_ref, lw_ref, lcb_ref, wax_ref,
                bax_ref, lam_ref, wout_ref, out_ref, pa_ref, pb_ref, y_ref, hc_ref, hu_ref, hx_ref):
    XB = 2 * D_CONV
    GC = 2 * D_CONV + D_LRU

    def reset():
        hc_ref[...] = jnp.zeros_like(hc_ref)
        hu_ref[...] = jnp.zeros_like(hu_ref)
        hx_ref[...] = jnp.zeros_like(hx_ref)

    def mix(p_ref, project_piece):
        lam = lam_ref[...]
        neg_c_softplus = -LRU_C * (jnp.maximum(-lam, 0.0) + jnp.log1p(jnp.exp(-jnp.abs(lam))))
        bax = bax_ref[...]
        sub = lax.broadcasted_iota(jnp.int32, (CHUNK, D_LRU), 0) & (SUBLANES - 1)
        row0 = lax.broadcasted_iota(jnp.int32, (CHUNK, D_LRU), 0) + _tile_row0()
        carry = hc_ref[0:1, :]

        for c in range(N_CHUNKS):
            project_piece(c)
            r0 = c * CHUNK
            prow = slice(r0, r0 + CHUNK)

            p_ref[prow, 0:D_CONV] = p_ref[prow, 0:D_CONV] * _sigmoid(p_ref[prow, D_CONV:2 * D_CONV])
            for gi in range(CONV_GROUPS):
                gs = slice(gi * LANES, (gi + 1) * LANES)
                if c == 0:
                    uw = jnp.concatenate([hu_ref[:, gs], p_ref[0:CHUNK, gs]], axis=0)
                else:
                    uw = p_ref[r0 - HIST:r0 + CHUNK, gs]
                acc = None
                for r in range(SUBLANES):
                    part = None
                    for a in range((CONV_WIDTH - 1 - r) // SUBLANES + 1):
                        j = CONV_WIDTH - 1 - (SUBLANES * a + r)
                        lo = HIST - SUBLANES * (a + 1)
                        term = cw_ref[j:j + 1, gs] * uw[lo:lo + CHUNK + SUBLANES]
                        part = term if part is None else part + term
                    if r:
                        part = pltpu.roll(part, r, axis=0)
                    acc = part if acc is None else acc + part
                ug = acc[SUBLANES:] + cb_ref[:, gs]
                mu = jnp.mean(ug, axis=-1, keepdims=True)
                dv = ug - mu
                var = jnp.mean(dv * dv, axis=-1, keepdims=True)
                yn = dv * lax.rsqrt(var + EPS) * lng_ref[:, gs] + lnb_ref[:, gs]
                y_ref[prow, gs] = _silu(yn).astype(BF16)

            if c == 0:
                xw = jnp.concatenate([hx_ref[HIST - SUBLANES:HIST, :], p_ref[0:CHUNK, XB:XB + D_LRU]], axis=0)
            else:
                xw = p_ref[r0 - SUBLANES:r0 + CHUNK, XB:XB + D_LRU]
            u = lw_ref[LRU_CONV - 1:LRU_CONV, :] * xw
            for d in range(1, LRU_CONV):
                u = u + lw_ref[LRU_CONV - 1 - d:LRU_CONV - d, :] * pltpu.roll(xw, d, axis=0)
            u = u[SUBLANES:] + lcb_ref[...]
            gates = _dot(u.astype(BF16), wax_ref[...]) + bax
            r = _sigmoid(gates[:, 0:D_LRU])
            ig = _sigmoid(gates[:, D_LRU:2 * D_LRU])
            log_a = r * neg_c_softplus
            a = jnp.exp(log_a)
            mult = jnp.sqrt(jnp.tanh(-log_a) * (a * a + 1.0))
            first = (row0 + r0) == N_PAD
            bt = jnp.where(first, 1.0, mult) * (ig * u)
            a = jnp.where(first, 0.0, a)
            for sh in (1, 2, 4):
                keep = sub >= sh
                a_sh = jnp.where(keep, pltpu.roll(a, sh, axis=0), 1.0)
                b_sh = jnp.where(keep, pltpu.roll(bt, sh, axis=0), 0.0)
                bt = bt + a * b_sh
                a = a * a_sh
            hs = []
            for blk in range(CHUNK // SUBLANES):
                rs = slice(blk * SUBLANES, (blk + 1) * SUBLANES)
                hb = bt[rs, :] + a[rs, :] * carry
                carry = hb[SUBLANES - 1:SUBLANES, :]
                hs.append(hb)
            hseq = jnp.concatenate(hs, axis=0)
            dg = p_ref[prow, GC:GC + D_LRU]
            gelu = 0.5 * dg * (1.0 + jnp.tanh(0.7978845608028654 * (dg + 0.044715 * dg * dg * dg)))
            y_ref[prow, D_CONV:D_CONV + D_LRU] = (gelu * hseq).astype(BF16)

        hc_ref[0:1, :] = carry
        hu_ref[...] = p_ref[TL_MIX - HIST:TL_MIX, 0:D_CONV]
        hx_ref[...] = p_ref[TL_MIX - HIST:TL_MIX, XB:XB + D_LRU]
        out_ref[...] = h_ref[...] + _dot(y_ref[...], wout_ref[...])

    _mixer_steps(h_ref, hn_ref, g_ref, win_ref, pa_ref, pb_ref, reset, mix)


def _odd_call(h, g, win, cw, cb, lng, lnb, lw, lcb, wax, bax, lam, wout):
    consts = (g, win, cw, cb, lng, lnb, lw, lcb, wax, bax, lam, wout)
    scratch = [pltpu.VMEM((SUBLANES, D_LRU), F32),
               pltpu.VMEM((HIST, D_CONV), F32),
               pltpu.VMEM((HIST, D_LRU), F32)]
    return _mixer_call(_odd_kernel, "odd_mixer", h, consts, D_IN_ODD, scratch)


def _block_diag(w):
    g, n, _ = w.shape
    eye = jnp.eye(g, dtype=w.dtype)
    return jnp.einsum('gij,gh->gihj', w, eye).reshape(g * n, g * n)


def kernel(x, meta_tokens, ffn1_norm, ffn1_wg, ffn1_wu, ffn1_wd, mix_norm, ffn2_norm, ffn2_wg, ffn2_wu, ffn2_wd, w_in_even, pool_w, pool_scale, hgrn_lb_logits, hgrn_gnorm, w_out_even, w_in_odd, conv_w, conv_b, conv_ln_g, conv_ln_b, lru_conv_w, lru_conv_b, lru_wa, lru_ba, lru_wx, lru_bx, lru_lambda, w_out_odd, final_norm):
    bn = x.shape[0]
    x_flat = x.astype(F32).reshape(bn * SEQ, D_MODEL)
    head = jnp.concatenate([jnp.zeros((N_PAD, D_MODEL), F32), meta_tokens.astype(F32)], axis=0)

    row = lambda a: a.reshape(1, -1).astype(F32)
    bf = lambda a: a.astype(BF16)

    def ffn_args(norm, wg, wu, wd, l):
        return (row(norm[l]), bf(wg[l]), bf(wu[l]), bf(wd[l]))

    tril = jnp.tril(jnp.ones((CHUNK, CHUNK), BF16))

    h = _ffn_first_call(x_flat, head, ffn_args(ffn1_norm, ffn1_wg, ffn1_wu, ffn1_wd, 0))
    h = _even_call(h, row(mix_norm[0]), bf(w_in_even[0]), bf(_block_diag(pool_w[0])), row(pool_scale[0]),
                   hgrn_lb_logits.astype(F32), row(hgrn_gnorm[0]), bf(w_out_even[0]), tril)
    h = _ffn_mid_call(h, ffn_args(ffn2_norm, ffn2_wg, ffn2_wu, ffn2_wd, 0))
    h = _ffn_mid_call(h, ffn_args(ffn1_norm, ffn1_wg, ffn1_wu, ffn1_wd, 1))
    wax = jnp.concatenate([_block_diag(lru_wa[0]), _block_diag(lru_wx[0])], axis=1)
    bax = jnp.concatenate([lru_ba[0], lru_bx[0]]).reshape(1, -1).astype(F32)
    h = _odd_call(h, row(mix_norm[1]), bf(w_in_odd[0]), conv_w[0].astype(F32), row(conv_b[0]),
                  row(conv_ln_g[0]), row(conv_ln_b[0]), lru_conv_w[0].astype(F32), row(lru_conv_b[0]),
                  bf(wax), bax, row(lru_lambda[0]), bf(w_out_odd[0]))
    out = _ffn_final_call(h, ffn_args(ffn2_norm, ffn2_wg, ffn2_wu, ffn2_wd, 1), row(final_norm))
    return out.reshape(bn, SEQ, D_MODEL)
```

```python
import functools

import jax
import jax.numpy as jnp
from jax import lax
from jax.experimental import pallas as pl
from jax.experimental.pallas import tpu as pltpu

F32 = jnp.float32
BF16 = jnp.bfloat16

D_MODEL = 1024
BATCH = 8
SEQ = 4096
N_META = 16
D_FF = 2816
EPS = 1e-6
D_POOL = 256
HG_HEAD = 128
D_HGRN = 768
HG_HEADS = 6
CHUNK = 64
D_IN_EVEN = D_POOL + 4 * D_HGRN
D_CONV = 512
CONV_WIDTH = 31
CONV_GROUPS = 4
D_LRU = 512
LRU_CONV = 4
LRU_C = 8.0
D_IN_ODD = 2 * D_CONV + 2 * D_LRU
LANES = 128
SUBLANES = 8

HEAD = CHUNK
N_PAD = HEAD - N_META
L_PAD = HEAD + SEQ
ROWS = BATCH * L_PAD

TM_FFN = 640
TM_OUT = 1024
FFN_SUBTILES = 2
FF_BOUNDS = (0, 1536, D_FF)
TL_MIX = 320
N_CHUNKS = TL_MIX // CHUNK
TILES_PER_SEQ = L_PAD // TL_MIX
N_TILES = ROWS // TL_MIX
HIST = 32
VMEM_LIMIT = 58 * 1024 * 1024


def _rms(x, g):
    ms = jnp.mean(x * x, axis=-1, keepdims=True)
    return x * lax.rsqrt(ms + EPS) * g


NEG_LOG2E = -1.4426950408889634


def _exp_neg(x):
    return jnp.exp2(x * NEG_LOG2E)


def _sigmoid(x):
    return 1.0 / (1.0 + _exp_neg(x))


def _silu(x):
    return x * _sigmoid(x)


def _dot(a, b):
    return jnp.dot(a, b, preferred_element_type=F32)


def _dot_nt(a, b):
    return lax.dot_general(a, b, (((1,), (1,)), ((), ())), preferred_element_type=F32)


def _dot_tn(a, b):
    return lax.dot_general(a, b, (((0,), (0,)), ((), ())), preferred_element_type=F32)


def _resident():
    return pl.BlockSpec(memory_space=pltpu.VMEM)


def _swiglu(xn, wg_ref, wu_ref, wd_ref):
    acc = None
    for lo, hi in zip(FF_BOUNDS[:-1], FF_BOUNDS[1:]):
        gate = _dot(xn, wg_ref[:, lo:hi])
        up = _dot(xn, wu_ref[:, lo:hi])
        hid = (_silu(gate) * up).astype(BF16)
        part = _dot(hid, wd_ref[lo:hi, :])
        acc = part if acc is None else acc + part
    return acc


def _ffn_kernel(splice_head, final, n_sub, *refs):
    out_ref = refs[-1]
    nxt = n_sub + 1 if splice_head else 1
    g_ref, wg_ref, wu_ref, wd_ref = refs[nxt:nxt + 4]
    sub = out_ref.shape[0] // n_sub
    for r in range(n_sub):
        rows = slice(r * sub, (r + 1) * sub)
        if splice_head:
            h = refs[r][...]
            first = (pl.program_id(0) * n_sub + r) % TILES_PER_SEQ == 0
            spliced = jnp.concatenate([refs[n_sub][...], h[0:TL_MIX - HEAD]], axis=0)
            h = jnp.where(first, spliced, h)
        else:
            h = refs[0][rows, :]
        xn = _rms(h, g_ref[...]).astype(BF16)
        h = h + 0.5 * _swiglu(xn, wg_ref, wu_ref, wd_ref)
        if final:
            h = _rms(h, refs[nxt + 4][...])
        out_ref[rows, :] = h


def _ffn_pallas(name, kernel_fn, grid, row_specs, out_spec, out_rows, args):
    return pl.pallas_call(
        kernel_fn,
        out_shape=jax.ShapeDtypeStruct((out_rows, D_MODEL), F32),
        grid=(grid,),
        in_specs=list(row_specs) + [_resident()] * (len(args) - len(row_specs)),
        out_specs=out_spec,
        compiler_params=pltpu.CompilerParams(
            dimension_semantics=("parallel",), vmem_limit_bytes=VMEM_LIMIT),
        name=name,
    )(*args)


def _x_tile_offset(n):
    return CHUNK * ((n // TILES_PER_SEQ) * (SEQ // CHUNK) + jnp.maximum((n % TILES_PER_SEQ) * N_CHUNKS - 1, 0))


def _ffn_first_call(x_flat, head, ffn):
    n_sub = FFN_SUBTILES
    in_specs = [pl.BlockSpec((pl.Element(TL_MIX), pl.Element(D_MODEL)),
                             lambda j, r=r: (_x_tile_offset(j * n_sub + r), 0)) for r in range(n_sub)]
    out_spec = pl.BlockSpec((n_sub * TL_MIX, D_MODEL), lambda j: (j, 0))
    return _ffn_pallas("ffn_first", functools.partial(_ffn_kernel, True, False, n_sub), N_TILES // n_sub, in_specs,
                       out_spec, ROWS, (x_flat,) * n_sub + (head,) + ffn)


def _ffn_mid_call(h, ffn):
    spec = pl.BlockSpec((TM_FFN, D_MODEL), lambda i: (i, 0))
    return _ffn_pallas("ffn_mid", functools.partial(_ffn_kernel, False, False, FFN_SUBTILES), ROWS // TM_FFN,
                       [spec], spec, ROWS, (h,) + ffn)


def _ffn_final_call(h, ffn, final_g):
    per_seq = SEQ // TM_OUT
    in_spec = pl.BlockSpec((pl.Element(TM_OUT), pl.Element(D_MODEL)),
                           lambda i: (CHUNK * ((i // per_seq) * (L_PAD // CHUNK) + HEAD // CHUNK
                                               + (i % per_seq) * (TM_OUT // CHUNK)), 0))
    out_spec = pl.BlockSpec((TM_OUT, D_MODEL), lambda i: (i, 0))
    return _ffn_pallas("ffn_final", functools.partial(_ffn_kernel, False, True, FFN_SUBTILES), BATCH * per_seq,
                       [in_spec], out_spec, BATCH * SEQ, (h,) + ffn + (final_g,))


def _mixer_steps(h_ref, hn_ref, g_ref, win_ref, pa_ref, pb_ref, reset, mix):
    s = pl.program_id(0)
    g = g_ref[...]
    d_in = win_ref.shape[1]
    n_col_tiles = d_in // (2 * LANES)
    bounds = [2 * LANES * ((n_col_tiles * i) // N_CHUNKS) for i in range(N_CHUNKS + 1)]

    @pl.when(s == 0)
    def _():
        pa_ref[...] = _dot(_rms(h_ref[...], g).astype(BF16), win_ref[...])

    @pl.when(s % TILES_PER_SEQ == 0)
    def _():
        reset()

    def step(cur_ref, nxt_ref):
        un = _rms(hn_ref[...], g).astype(BF16)

        def project_piece(c):
            lo, hi = bounds[c], bounds[c + 1]
            if hi > lo:
                nxt_ref[:, lo:hi] = _dot(un, win_ref[:, lo:hi])

        mix(cur_ref, project_piece)

    @pl.when(s % 2 == 0)
    def _():
        step(pa_ref, pb_ref)

    @pl.when(s % 2 == 1)
    def _():
        step(pb_ref, pa_ref)


def _mixer_call(body, name, h, consts, d_in, scratch):
    cur_spec = pl.BlockSpec((TL_MIX, D_MODEL), lambda s: (s, 0))
    nxt_spec = pl.BlockSpec((TL_MIX, D_MODEL), lambda s: (jnp.minimum(s + 1, N_TILES - 1), 0))
    return pl.pallas_call(
        body,
        out_shape=jax.ShapeDtypeStruct((ROWS, D_MODEL), F32),
        grid=(N_TILES,),
        in_specs=[cur_spec, nxt_spec] + [_resident()] * len(consts),
        out_specs=cur_spec,
        scratch_shapes=[pltpu.VMEM((TL_MIX, d_in), F32),
                        pltpu.VMEM((TL_MIX, d_in), F32),
                        pltpu.VMEM((TL_MIX, D_MODEL), BF16),
                        ] + scratch,
        compiler_params=pltpu.CompilerParams(
            dimension_semantics=("arbitrary",), vmem_limit_bytes=VMEM_LIMIT),
        name=name,
    )(h, h, *consts)


def _tile_row0():
    return (pl.program_id(0) % TILES_PER_SEQ) * TL_MIX


DECAY_SAFE = 150.0


def _even_kernel(h_ref, hn_ref, g_ref, win_ref, poolw_ref, pscale_ref, lbl_ref, gn_ref, wout_ref, tril_ref,
                 out_ref, pa_ref, pb_ref, y_ref, st_ref, hist_ref, qt_ref, kt_ref, qi_ref, ks_ref, dec_ref, a_ref,
                 sprev_ref, bs_ref, kk_ref):
    QC = D_POOL
    FC = D_POOL + D_HGRN
    VC = D_POOL + 2 * D_HGRN
    GC = D_POOL + 3 * D_HGRN

    def reset():
        st_ref[...] = jnp.zeros_like(st_ref)
        hist_ref[...] = jnp.zeros_like(hist_ref)

    def mix(p_ref, project_piece):
        lbl = lbl_ref[...]
        mx = jnp.maximum(lbl[0:1, :], lbl[1:2, :])
        e0 = jnp.exp(lbl[0:1, :] - mx)
        e1 = jnp.exp(lbl[1:2, :] - mx)
        lb = e0 / (e0 + e1)
        log_lb = jnp.log(lb)
        log_1mlb = jnp.log1p(-lb)
        one_mlb = 1.0 - lb
        gn = gn_ref[...]
        tril = tril_ref[...]
        causal = tril > 0.5
        pscale = pscale_ref[...]
        lane = lax.broadcasted_iota(jnp.int32, (CHUNK, D_POOL), 1)
        win = jnp.where(lane < 64, 2.0, jnp.where(lane < 128, 4.0, jnp.where(lane < 192, 8.0, 16.0)))
        pos = lax.broadcasted_iota(jnp.int32, (CHUNK, D_POOL), 0) + (_tile_row0() + 1 - N_PAD)

        def gates(prow):
            q = _silu(p_ref[prow, QC:QC + D_HGRN])
            z = p_ref[prow, FC:FC + D_HGRN]
            ez = _exp_neg(jnp.abs(z))
            wz = 1.0 + ez
            log_sig = jnp.minimum(z, 0.0) - jnp.log(wz)
            rz = 1.0 / wz
            k = one_mlb * jnp.where(z >= 0, ez * rz, rz)
            bb = log_1mlb + log_sig
            log_f = jnp.maximum(log_lb, bb) + jnp.log(1.0 + _exp_neg(jnp.abs(log_lb - bb)))
            lf_hi = log_f.astype(BF16)
            lf_lo = (log_f - lf_hi.astype(F32)).astype(BF16)
            b = _dot(tril, lf_hi) + _dot(tril, lf_lo)
            return q, k, b

        worst = jnp.zeros((1, D_HGRN), F32)
        for c in range(N_CHUNKS):
            project_piece(c)
            r0 = c * CHUNK
            prow = slice(r0, r0 + CHUNK)

            if c == 0:
                w = jnp.concatenate([hist_ref[HIST - 16:HIST, :], p_ref[0:CHUNK, 0:D_POOL]], axis=0)
            else:
                w = p_ref[r0 - 16:r0 + CHUNK, 0:D_POOL]
            s2 = w + pltpu.roll(w, 1, axis=0)
            s4 = s2 + pltpu.roll(s2, 2, axis=0)
            s8 = s4 + pltpu.roll(s4, 4, axis=0)
            s16 = s8 + pltpu.roll(s8, 8, axis=0)
            x = w[16:]
            ssel = jnp.where(lane < 64, s2[16:],
                             jnp.where(lane < 128, s4[16:], jnp.where(lane < 192, s8[16:], s16[16:])))
            cnt = jnp.minimum(jnp.maximum(pos + r0, 1).astype(F32), win)
            mixed = ssel / cnt - x
            ya = _dot(mixed.astype(BF16), poolw_ref[...]) * pscale
            y_ref[prow, 0:D_POOL] = ya.astype(BF16)

            q, k, b = gates(prow)
            b_end = b[CHUNK - 1:CHUNK, :]
            bm = b - 0.5 * b_end
            qt_ref[prow, :] = (q * jnp.exp(bm)).astype(BF16)
            kt_ref[prow, :] = (k * _exp_neg(bm)).astype(BF16)
            qi_ref[prow, :] = (q * jnp.exp(b)).astype(BF16)
            ks_ref[prow, :] = (k * jnp.exp(b_end - b)).astype(BF16)
            dec_ref[c:c + 1, :] = jnp.exp(b_end)
            worst = jnp.maximum(worst, -b_end)

        for c in range(N_CHUNKS):
            prow = slice(c * CHUNK, (c + 1) * CHUNK)
            for hd in range(HG_HEADS):
                hs = slice(hd * HG_HEAD, (hd + 1) * HG_HEAD)
                a = jnp.where(causal, _dot_nt(qt_ref[prow, hs], kt_ref[prow, hs]), 0.0)
                a_ref[hd, prow, 0:CHUNK] = a.astype(BF16)
                v = p_ref[prow, VC + hd * HG_HEAD:VC + (hd + 1) * HG_HEAD].astype(BF16)
                st = st_ref[hd]
                sprev_ref[c * HG_HEADS + hd] = st.astype(BF16)
                st_ref[hd] = st * dec_ref[c:c + 1, hs] + _dot_tn(v, ks_ref[prow, hs])

        @pl.when(jnp.max(worst) > DECAY_SAFE)
        def _():
            col = lax.broadcasted_iota(jnp.int32, (CHUNK, CHUNK), 1)
            for c in range(N_CHUNKS):
                prow = slice(c * CHUNK, (c + 1) * CHUNK)
                q, k, b = gates(prow)
                bs_ref[...] = b
                kk_ref[...] = k

                def column(s, accs):
                    wgt = q * jnp.exp(jnp.minimum(b - bs_ref[pl.ds(s, 1), :], 0.0)) * kk_ref[pl.ds(s, 1), :]
                    out = []
                    for hd in range(HG_HEADS):
                        d = jnp.sum(wgt[:, hd * HG_HEAD:(hd + 1) * HG_HEAD], axis=-1, keepdims=True)
                        out.append(jnp.where(col == s, d, accs[hd]))
                    return tuple(out)

                accs = lax.fori_loop(0, CHUNK, column, tuple(jnp.zeros((CHUNK, CHUNK), F32) for _ in range(HG_HEADS)))
                for hd in range(HG_HEADS):
                    a_ref[hd, prow, 0:CHUNK] = jnp.where(causal, accs[hd], 0.0).astype(BF16)

        for c in range(N_CHUNKS):
            prow = slice(c * CHUNK, (c + 1) * CHUNK)
            for hd in range(HG_HEADS):
                hs = slice(hd * HG_HEAD, (hd + 1) * HG_HEAD)
                v = p_ref[prow, VC + hd * HG_HEAD:VC + (hd + 1) * HG_HEAD].astype(BF16)
                o = _dot(a_ref[hd, prow, 0:CHUNK], v) + _dot_nt(qi_ref[prow, hs], sprev_ref[c * HG_HEADS + hd])
                o = o * lax.rsqrt(jnp.mean(o * o, axis=-1, keepdims=True) + EPS) * gn
                gg = p_ref[prow, GC + hd * HG_HEAD:GC + (hd + 1) * HG_HEAD]
                y_ref[prow, D_POOL + hd * HG_HEAD:D_POOL + (hd + 1) * HG_HEAD] = (o * _silu(gg)).astype(BF16)

        hist_ref[...] = p_ref[TL_MIX - HIST:TL_MIX, 0:D_POOL]
        out_ref[...] = h_ref[...] + _dot(y_ref[...], wout_ref[...])

    _mixer_steps(h_ref, hn_ref, g_ref, win_ref, pa_ref, pb_ref, reset, mix)


def _even_call(h, g, win, poolw, pscale, lbl, gn, wout, tril):
    consts = (g, win, poolw, pscale, lbl, gn, wout, tril)
    scratch = [pltpu.VMEM((HG_HEADS, HG_HEAD, HG_HEAD), F32),
               pltpu.VMEM((HIST, D_POOL), F32),
               pltpu.VMEM((TL_MIX, D_HGRN), BF16),
               pltpu.VMEM((TL_MIX, D_HGRN), BF16),
               pltpu.VMEM((TL_MIX, D_HGRN), BF16),
               pltpu.VMEM((TL_MIX, D_HGRN), BF16),
               pltpu.VMEM((SUBLANES, D_HGRN), F32),
               pltpu.VMEM((HG_HEADS, TL_MIX, LANES), BF16),
               pltpu.VMEM((N_CHUNKS * HG_HEADS, HG_HEAD, HG_HEAD), BF16),
               pltpu.VMEM((CHUNK, D_HGRN), F32),
               pltpu.VMEM((CHUNK, D_HGRN), F32)]
    return _mixer_call(_even_kernel, "even_mixer", h, consts, D_IN_EVEN, scratch)


def _odd_kernel(h_ref, hn_ref, g_ref, win_ref, cw_ref, cb_ref, lng_ref, lnb_ref, lw_ref, lcb_ref, wax_ref,
                bax_ref, lam_ref, wout_ref, out_ref, pa_ref, pb_ref, y_ref, hc_ref, hu_ref, hx_ref):
    XB = 2 * D_CONV
    GC = 2 * D_CONV + D_LRU

    def reset():
        hc_ref[...] = jnp.zeros_like(hc_ref)
        hu_ref[...] = jnp.zeros_like(hu_ref)
        hx_ref[...] = jnp.zeros_like(hx_ref)

    def mix(p_ref, project_piece):
        lam = lam_ref[...]
        neg_c_softplus = -LRU_C * (jnp.maximum(-lam, 0.0) + jnp.log1p(jnp.exp(-jnp.abs(lam))))
        bax = bax_ref[...]
        sub = lax.broadcasted_iota(jnp.int32, (CHUNK, D_LRU), 0) & (SUBLANES - 1)
        row0 = lax.broadcasted_iota(jnp.int32, (CHUNK, D_LRU), 0) + _tile_row0()
        carry = hc_ref[0:1, :]

        for c in range(N_CHUNKS):
            project_piece(c)
            r0 = c * CHUNK
            prow = slice(r0, r0 + CHUNK)

            p_ref[prow, 0:D_CONV] = p_ref[prow, 0:D_CONV] * _sigmoid(p_ref[prow, D_CONV:2 * D_CONV])
            for gi in range(CONV_GROUPS):
                gs = slice(gi * LANES, (gi + 1) * LANES)
                if c == 0:
                    uw = jnp.concatenate([hu_ref[:, gs], p_ref[0:CHUNK, gs]], axis=0)
                else:
                    uw = p_ref[r0 - HIST:r0 + CHUNK, gs]
                acc = None
                for r in range(SUBLANES):
                    part = None
                    for a in range((CONV_WIDTH - 1 - r) // SUBLANES + 1):
                        j = CONV_WIDTH - 1 - (SUBLANES * a + r)
                        lo = HIST - SUBLANES * (a + 1)
                        term = cw_ref[j:j + 1, gs] * uw[lo:lo + CHUNK + SUBLANES]
                        part = term if part is None else part + term
                    if r:
                        part = pltpu.roll(part, r, axis=0)
                    acc = part if acc is None else acc + part
                ug = acc[SUBLANES:] + cb_ref[:, gs]
                mu = jnp.mean(ug, axis=-1, keepdims=True)
                dv = ug - mu
                var = jnp.mean(dv * dv, axis=-1, keepdims=True)
                yn = dv * lax.rsqrt(var + EPS) * lng_ref[:, gs] + lnb_ref[:, gs]
                y_ref[prow, gs] = _silu(yn).astype(BF16)

            if c == 0:
                xw = jnp.concatenate([hx_ref[HIST - SUBLANES:HIST, :], p_ref[0:CHUNK, XB:XB + D_LRU]], axis=0)
            else:
                xw = p_ref[r0 - SUBLANES:r0 + CHUNK, XB:XB + D_LRU]
            u = lw_ref[LRU_CONV - 1:LRU_CONV, :] * xw
            for d in range(1, LRU_CONV):
                u = u + lw_ref[LRU_CONV - 1 - d:LRU_CONV - d, :] * pltpu.roll(xw, d, axis=0)
            u = u[SUBLANES:] + lcb_ref[...]
            gates = _dot(u.astype(BF16), wax_ref[...]) + bax
            r = _sigmoid(gates[:, 0:D_LRU])
            ig = _sigmoid(gates[:, D_LRU:2 * D_LRU])
            log_a = r * neg_c_softplus
            a = jnp.exp(log_a)
            mult = jnp.sqrt(jnp.tanh(-log_a) * (a * a + 1.0))
            first = (row0 + r0) == N_PAD
            bt = jnp.where(first, 1.0, mult) * (ig * u)
            a = jnp.where(first, 0.0, a)
            for sh in (1, 2, 4):
                keep = sub >= sh
                a_sh = jnp.where(keep, pltpu.roll(a, sh, axis=0), 1.0)
                b_sh = jnp.where(keep, pltpu.roll(bt, sh, axis=0), 0.0)
                bt = bt + a * b_sh
                a = a * a_sh
            hs = []
            for blk in range(CHUNK // SUBLANES):
                rs = slice(blk * SUBLANES, (blk + 1) * SUBLANES)
                hb = bt[rs, :] + a[rs, :] * carry
                carry = hb[SUBLANES - 1:SUBLANES, :]
                hs.append(hb)
            hseq = jnp.concatenate(hs, axis=0)
            dg = p_ref[prow, GC:GC + D_LRU]
            gelu = 0.5 * dg * (1.0 + jnp.tanh(0.7978845608028654 * (dg + 0.044715 * dg * dg * dg)))
            y_ref[prow, D_CONV:D_CONV + D_LRU] = (gelu * hseq).astype(BF16)

        hc_ref[0:1, :] = carry
        hu_ref[...] = p_ref[TL_MIX - HIST:TL_MIX, 0:D_CONV]
        hx_ref[...] = p_ref[TL_MIX - HIST:TL_MIX, XB:XB + D_LRU]
        out_ref[...] = h_ref[...] + _dot(y_ref[...], wout_ref[...])

    _mixer_steps(h_ref, hn_ref, g_ref, win_ref, pa_ref, pb_ref, reset, mix)


def _odd_call(h, g, win, cw, cb, lng, lnb, lw, lcb, wax, bax, lam, wout):
    consts = (g, win, cw, cb, lng, lnb, lw, lcb, wax, bax, lam, wout)
    scratch = [pltpu.VMEM((SUBLANES, D_LRU), F32),
               pltpu.VMEM((HIST, D_CONV), F32),
               pltpu.VMEM((HIST, D_LRU), F32)]
    return _mixer_call(_odd_kernel, "odd_mixer", h, consts, D_IN_ODD, scratch)


def _block_diag(w):
    g, n, _ = w.shape
    eye = jnp.eye(g, dtype=w.dtype)
    return jnp.einsum('gij,gh->gihj', w, eye).reshape(g * n, g * n)


def kernel(x, meta_tokens, ffn1_norm, ffn1_wg, ffn1_wu, ffn1_wd, mix_norm, ffn2_norm, ffn2_wg, ffn2_wu, ffn2_wd, w_in_even, pool_w, pool_scale, hgrn_lb_logits, hgrn_gnorm, w_out_even, w_in_odd, conv_w, conv_b, conv_ln_g, conv_ln_b, lru_conv_w, lru_conv_b, lru_wa, lru_ba, lru_wx, lru_bx, lru_lambda, w_out_odd, final_norm):
    bn = x.shape[0]
    x_flat = x.astype(F32).reshape(bn * SEQ, D_MODEL)
    head = jnp.concatenate([jnp.zeros((N_PAD, D_MODEL), F32), meta_tokens.astype(F32)], axis=0)

    row = lambda a: a.reshape(1, -1).astype(F32)
    bf = lambda a: a.astype(BF16)

    def ffn_args(norm, wg, wu, wd, l):
        return (row(norm[l]), bf(wg[l]), bf(wu[l]), bf(wd[l]))

    tril = jnp.tril(jnp.ones((CHUNK, CHUNK), BF16))

    h = _ffn_first_call(x_flat, head, ffn_args(ffn1_norm, ffn1_wg, ffn1_wu, ffn1_wd, 0))
    h = _even_call(h, row(mix_norm[0]), bf(w_in_even[0]), bf(_block_diag(pool_w[0])), row(pool_scale[0]),
                   hgrn_lb_logits.astype(F32), row(hgrn_gnorm[0]), bf(w_out_even[0]), tril)
    h = _ffn_mid_call(h, ffn_args(ffn2_norm, ffn2_wg, ffn2_wu, ffn2_wd, 0))
    h = _ffn_mid_call(h, ffn_args(ffn1_norm, ffn1_wg, ffn1_wu, ffn1_wd, 1))
    wax = jnp.concatenate([_block_diag(lru_wa[0]), _block_diag(lru_wx[0])], axis=1)
    bax = jnp.concatenate([lru_ba[0], lru_bx[0]]).reshape(1, -1).astype(F32)
    h = _odd_call(h, row(mix_norm[1]), bf(w_in_odd[0]), conv_w[0].astype(F32), row(conv_b[0]),
                  row(conv_ln_g[0]), row(conv_ln_b[0]), lru_conv_w[0].astype(F32), row(lru_conv_b[0]),
                  bf(wax), bax, row(lru_lambda[0]), bf(w_out_odd[0]))
    out = _ffn_final_call(h, ffn_args(ffn2_norm, ffn2_wg, ffn2_wu, ffn2_wd, 1), row(final_norm))
    return out.reshape(bn, SEQ, D_MODEL)
```

```python
import functools

import jax
import jax.numpy as jnp
from jax import lax
from jax.experimental import pallas as pl
from jax.experimental.pallas import tpu as pltpu

F32 = jnp.float32
BF16 = jnp.bfloat16

D_MODEL = 1024
BATCH = 8
SEQ = 4096
N_META = 16
D_FF = 2816
EPS = 1e-6
D_POOL = 256
HG_HEAD = 128
D_HGRN = 768
HG_HEADS = 6
CHUNK = 64
D_IN_EVEN = D_POOL + 4 * D_HGRN
D_CONV = 512
CONV_WIDTH = 31
CONV_GROUPS = 4
D_LRU = 512
LRU_CONV = 4
LRU_C = 8.0
D_IN_ODD = 2 * D_CONV + 2 * D_LRU
LANES = 128
SUBLANES = 8

HEAD = CHUNK
N_PAD = HEAD - N_META
L_PAD = HEAD + SEQ
ROWS = BATCH * L_PAD

TM_FFN = 640
TM_OUT = 512
FFN_SUBTILES = 2
FF_BOUNDS = (0, 1536, D_FF)
TL_MIX = 320
N_CHUNKS = TL_MIX // CHUNK
TILES_PER_SEQ = L_PAD // TL_MIX
N_TILES = ROWS // TL_MIX
HIST = 32
VMEM_LIMIT = 58 * 1024 * 1024


def _rms(x, g):
    ms = jnp.mean(x * x, axis=-1, keepdims=True)
    return x * lax.rsqrt(ms + EPS) * g


NEG_LOG2E = -1.4426950408889634


def _exp_neg(x):
    return jnp.exp2(x * NEG_LOG2E)


def _sigmoid(x):
    return 1.0 / (1.0 + _exp_neg(x))


def _silu(x):
    return x * _sigmoid(x)


def _dot(a, b):
    return jnp.dot(a, b, preferred_element_type=F32)


def _dot_nt(a, b):
    return lax.dot_general(a, b, (((1,), (1,)), ((), ())), preferred_element_type=F32)


def _dot_tn(a, b):
    return lax.dot_general(a, b, (((0,), (0,)), ((), ())), preferred_element_type=F32)


def _resident():
    return pl.BlockSpec(memory_space=pltpu.VMEM)


N_WSTEPS = 8
WG_SLAB = D_MODEL // N_WSTEPS
WD_SLAB = D_FF // N_WSTEPS


def _swiglu(xn, wg_ref, wu_ref, wd_ref):
    acc = None
    for lo, hi in zip(FF_BOUNDS[:-1], FF_BOUNDS[1:]):
        gate = _dot(xn, wg_ref[:, lo:hi])
        up = _dot(xn, wu_ref[:, lo:hi])
        hid = (_silu(gate) * up).astype(BF16)
        part = _dot(hid, wd_ref[lo:hi, :])
        acc = part if acc is None else acc + part
    return acc


def _ffn_kernel(splice_head, final, n_sub, *refs):
    wg_ref, wu_ref, wd_ref = refs[-3:]
    out_ref = refs[-4]
    nxt = n_sub + 1 if splice_head else 1
    g_ref, wg_slab, wu_slab, wd_slab = refs[nxt:nxt + 4]
    step = pl.program_id(0)

    @pl.when(step < N_WSTEPS)
    def _():
        rg = pl.ds(pl.multiple_of(step * WG_SLAB, WG_SLAB), WG_SLAB)
        wg_ref[rg, :] = wg_slab[...].astype(BF16)
        wu_ref[rg, :] = wu_slab[...].astype(BF16)
        wd_ref[pl.ds(pl.multiple_of(step * WD_SLAB, 16), WD_SLAB), :] = wd_slab[...].astype(BF16)

    @pl.when(step >= N_WSTEPS)
    def _():
        sub = out_ref.shape[0] // n_sub
        for r in range(n_sub):
            rows = slice(r * sub, (r + 1) * sub)
            if splice_head:
                h = refs[r][...]
                first = ((step - N_WSTEPS) * n_sub + r) % TILES_PER_SEQ == 0
                spliced = jnp.concatenate([refs[n_sub][...], h[0:TL_MIX - HEAD]], axis=0)
                h = jnp.where(first, spliced, h)
            else:
                h = refs[0][rows, :]
            xn = _rms(h, g_ref[...]).astype(BF16)
            h = h + 0.5 * _swiglu(xn, wg_ref, wu_ref, wd_ref)
            if final:
                h = _rms(h, refs[nxt + 4][...])
            out_ref[rows, :] = h


def _tile_of(step):
    return jnp.maximum(step - N_WSTEPS, 0)


def _ffn_pallas(name, kernel_fn, n_tiles, row_specs, out_spec, out_rows, row_args, layer, g, wg, wu, wd, extra):
    slab = lambda s: jnp.minimum(s, N_WSTEPS - 1)
    w_specs = [pl.BlockSpec((None, WG_SLAB, D_FF), lambda s: (layer, slab(s), 0)),
               pl.BlockSpec((None, WG_SLAB, D_FF), lambda s: (layer, slab(s), 0)),
               pl.BlockSpec((None, WD_SLAB, D_MODEL), lambda s: (layer, slab(s), 0))]
    n_res = len(row_args) - len(row_specs)
    return pl.pallas_call(
        kernel_fn,
        out_shape=jax.ShapeDtypeStruct((out_rows, D_MODEL), F32),
        grid=(N_WSTEPS + n_tiles,),
        in_specs=list(row_specs) + [_resident()] * (n_res + 1) + w_specs + [_resident()] * len(extra),
        out_specs=out_spec,
        scratch_shapes=[pltpu.VMEM((D_MODEL, D_FF), BF16), pltpu.VMEM((D_MODEL, D_FF), BF16),
                        pltpu.VMEM((D_FF, D_MODEL), BF16)],
        compiler_params=pltpu.CompilerParams(
            dimension_semantics=("arbitrary",), vmem_limit_bytes=VMEM_LIMIT),
        name=name,
    )(*row_args, g, wg, wu, wd, *extra)


def _x_tile_offset(n):
    return CHUNK * ((n // TILES_PER_SEQ) * (SEQ // CHUNK) + jnp.maximum((n % TILES_PER_SEQ) * N_CHUNKS - 1, 0))


def _ffn_first_call(x_flat, head, layer, g, wg, wu, wd):
    n_sub = FFN_SUBTILES
    in_specs = [pl.BlockSpec((pl.Element(TL_MIX), pl.Element(D_MODEL)),
                             lambda s, r=r: (_x_tile_offset(_tile_of(s) * n_sub + r), 0)) for r in range(n_sub)]
    out_spec = pl.BlockSpec((n_sub * TL_MIX, D_MODEL), lambda s: (_tile_of(s), 0))
    return _ffn_pallas("ffn_first", functools.partial(_ffn_kernel, True, False, n_sub), N_TILES // n_sub, in_specs,
                       out_spec, ROWS, (x_flat,) * n_sub + (head,), layer, g, wg, wu, wd, ())


def _ffn_mid_call(h, layer, g, wg, wu, wd):
    spec = pl.BlockSpec((TM_FFN, D_MODEL), lambda s: (_tile_of(s), 0))
    return _ffn_pallas("ffn_mid", functools.partial(_ffn_kernel, False, False, FFN_SUBTILES), ROWS // TM_FFN,
                       [spec], spec, ROWS, (h,), layer, g, wg, wu, wd, ())


def _ffn_final_call(h, layer, g, wg, wu, wd, final_g):
    per_seq = SEQ // TM_OUT

    def in_row(s):
        i = _tile_of(s)
        return CHUNK * ((i // per_seq) * (L_PAD // CHUNK) + HEAD // CHUNK + (i % per_seq) * (TM_OUT // CHUNK))

    in_spec = pl.BlockSpec((pl.Element(TM_OUT), pl.Element(D_MODEL)), lambda s: (in_row(s), 0))
    out_spec = pl.BlockSpec((TM_OUT, D_MODEL), lambda s: (_tile_of(s), 0))
    return _ffn_pallas("ffn_final", functools.partial(_ffn_kernel, False, True, FFN_SUBTILES), BATCH * per_seq,
                       [in_spec], out_spec, BATCH * SEQ, (h,), layer, g, wg, wu, wd, (final_g,))


def _mixer_steps(h_ref, hn_ref, g_ref, win_ref, pa_ref, pb_ref, reset, mix):
    s = pl.program_id(0)
    g = g_ref[...]
    d_in = win_ref.shape[1]
    n_col_tiles = d_in // (2 * LANES)
    bounds = [2 * LANES * ((n_col_tiles * i) // N_CHUNKS) for i in range(N_CHUNKS + 1)]

    @pl.when(s == 0)
    def _():
        pa_ref[...] = _dot(_rms(h_ref[...], g).astype(BF16), win_ref[...])

    @pl.when(s % TILES_PER_SEQ == 0)
    def _():
        reset()

    def step(cur_ref, nxt_ref):
        un = _rms(hn_ref[...], g).astype(BF16)

        def project_piece(c):
            lo, hi = bounds[c], bounds[c + 1]
            if hi > lo:
                nxt_ref[:, lo:hi] = _dot(un, win_ref[:, lo:hi])

        mix(cur_ref, project_piece)

    @pl.when(s % 2 == 0)
    def _():
        step(pa_ref, pb_ref)

    @pl.when(s % 2 == 1)
    def _():
        step(pb_ref, pa_ref)


def _mixer_call(body, name, h, consts, d_in, scratch):
    cur_spec = pl.BlockSpec((TL_MIX, D_MODEL), lambda s: (s, 0))
    nxt_spec = pl.BlockSpec((TL_MIX, D_MODEL), lambda s: (jnp.minimum(s + 1, N_TILES - 1), 0))
    return pl.pallas_call(
        body,
        out_shape=jax.ShapeDtypeStruct((ROWS, D_MODEL), F32),
        grid=(N_TILES,),
        in_specs=[cur_spec, nxt_spec] + [_resident()] * len(consts),
        out_specs=cur_spec,
        scratch_shapes=[pltpu.VMEM((TL_MIX, d_in), F32),
                        pltpu.VMEM((TL_MIX, d_in), F32),
                        pltpu.VMEM((TL_MIX, D_MODEL), BF16),
                        ] + scratch,
        compiler_params=pltpu.CompilerParams(
            dimension_semantics=("arbitrary",), vmem_limit_bytes=VMEM_LIMIT),
        name=name,
    )(h, h, *consts)


def _tile_row0():
    return (pl.program_id(0) % TILES_PER_SEQ) * TL_MIX


DECAY_SAFE = 150.0


def _even_kernel(h_ref, hn_ref, g_ref, win_ref, poolw_ref, pscale_ref, lbl_ref, gn_ref, wout_ref, tril_ref,
                 out_ref, pa_ref, pb_ref, y_ref, st_ref, hist_ref, qt_ref, kt_ref, qi_ref, ks_ref, dec_ref, a_ref,
                 sprev_ref, bs_ref, kk_ref):
    QC = D_POOL
    FC = D_POOL + D_HGRN
    VC = D_POOL + 2 * D_HGRN
    GC = D_POOL + 3 * D_HGRN

    def reset():
        st_ref[...] = jnp.zeros_like(st_ref)
        hist_ref[...] = jnp.zeros_like(hist_ref)

    def mix(p_ref, project_piece):
        lbl = lbl_ref[...]
        mx = jnp.maximum(lbl[0:1, :], lbl[1:2, :])
        e0 = jnp.exp(lbl[0:1, :] - mx)
        e1 = jnp.exp(lbl[1:2, :] - mx)
        lb = e0 / (e0 + e1)
        log_lb = jnp.log(lb)
        log_1mlb = jnp.log1p(-lb)
        one_mlb = 1.0 - lb
        gn = gn_ref[...]
        tril = tril_ref[...]
        causal = tril > 0.5
        pscale = pscale_ref[...]
        lane = lax.broadcasted_iota(jnp.int32, (CHUNK, D_POOL), 1)
        win = jnp.where(lane < 64, 2.0, jnp.where(lane < 128, 4.0, jnp.where(lane < 192, 8.0, 16.0)))
        pos = lax.broadcasted_iota(jnp.int32, (CHUNK, D_POOL), 0) + (_tile_row0() + 1 - N_PAD)

        def gates(prow):
            q = _silu(p_ref[prow, QC:QC + D_HGRN])
            z = p_ref[prow, FC:FC + D_HGRN]
            ez = _exp_neg(jnp.abs(z))
            wz = 1.0 + ez
            log_sig = jnp.minimum(z, 0.0) - jnp.log(wz)
            rz = 1.0 / wz
            k = one_mlb * jnp.where(z >= 0, ez * rz, rz)
            bb = log_1mlb + log_sig
            log_f = jnp.maximum(log_lb, bb) + jnp.log(1.0 + _exp_neg(jnp.abs(log_lb - bb)))
            lf_hi = log_f.astype(BF16)
            lf_lo = (log_f - lf_hi.astype(F32)).astype(BF16)
            b = _dot(tril, lf_hi) + _dot(tril, lf_lo)
            return q, k, b

        worst = jnp.zeros((1, D_HGRN), F32)
        for c in range(N_CHUNKS):
            project_piece(c)
            r0 = c * CHUNK
            prow = slice(r0, r0 + CHUNK)

            if c == 0:
                w = jnp.concatenate([hist_ref[HIST - 16:HIST, :], p_ref[0:CHUNK, 0:D_POOL]], axis=0)
            else:
                w = p_ref[r0 - 16:r0 + CHUNK, 0:D_POOL]
            s2 = w + pltpu.roll(w, 1, axis=0)
            s4 = s2 + pltpu.roll(s2, 2, axis=0)
            s8 = s4 + pltpu.roll(s4, 4, axis=0)
            s16 = s8 + pltpu.roll(s8, 8, axis=0)
            x = w[16:]
            ssel = jnp.where(lane < 64, s2[16:],
                             jnp.where(lane < 128, s4[16:], jnp.where(lane < 192, s8[16:], s16[16:])))
            cnt = jnp.minimum(jnp.maximum(pos + r0, 1).astype(F32), win)
            mixed = ssel / cnt - x
            ya = _dot(mixed.astype(BF16), poolw_ref[...]) * pscale
            y_ref[prow, 0:D_POOL] = ya.astype(BF16)

            q, k, b = gates(prow)
            b_end = b[CHUNK - 1:CHUNK, :]
            bm = b - 0.5 * b_end
            qt_ref[prow, :] = (q * jnp.exp(bm)).astype(BF16)
            kt_ref[prow, :] = (k * _exp_neg(bm)).astype(BF16)
            qi_ref[prow, :] = (q * jnp.exp(b)).astype(BF16)
            ks_ref[prow, :] = (k * jnp.exp(b_end - b)).astype(BF16)
            dec_ref[c:c + 1, :] = jnp.exp(b_end)
            worst = jnp.maximum(worst, -b_end)

        for c in range(N_CHUNKS):
            prow = slice(c * CHUNK, (c + 1) * CHUNK)
            for hd in range(HG_HEADS):
                hs = slice(hd * HG_HEAD, (hd + 1) * HG_HEAD)
                a = jnp.where(causal, _dot_nt(qt_ref[prow, hs], kt_ref[prow, hs]), 0.0)
                a_ref[hd, prow, 0:CHUNK] = a.astype(BF16)
                v = p_ref[prow, VC + hd * HG_HEAD:VC + (hd + 1) * HG_HEAD].astype(BF16)
                st = st_ref[hd]
                sprev_ref[c * HG_HEADS + hd] = st.astype(BF16)
                st_ref[hd] = st * dec_ref[c:c + 1, hs] + _dot_tn(v, ks_ref[prow, hs])

        @pl.when(jnp.max(worst) > DECAY_SAFE)
        def _():
            col = lax.broadcasted_iota(jnp.int32, (CHUNK, CHUNK), 1)
            for c in range(N_CHUNKS):
                prow = slice(c * CHUNK, (c + 1) * CHUNK)
                q, k, b = gates(prow)
                bs_ref[...] = b
                kk_ref[...] = k

                def column(s, accs):
                    wgt = q * jnp.exp(jnp.minimum(b - bs_ref[pl.ds(s, 1), :], 0.0)) * kk_ref[pl.ds(s, 1), :]
                    out = []
                    for hd in range(HG_HEADS):
                        d = jnp.sum(wgt[:, hd * HG_HEAD:(hd + 1) * HG_HEAD], axis=-1, keepdims=True)
                        out.append(jnp.where(col == s, d, accs[hd]))
                    return tuple(out)

                accs = lax.fori_loop(0, CHUNK, column, tuple(jnp.zeros((CHUNK, CHUNK), F32) for _ in range(HG_HEADS)))
                for hd in range(HG_HEADS):
                    a_ref[hd, prow, 0:CHUNK] = jnp.where(causal, accs[hd], 0.0).astype(BF16)

        for c in range(N_CHUNKS):
            prow = slice(c * CHUNK, (c + 1) * CHUNK)
            for hd in range(HG_HEADS):
                hs = slice(hd * HG_HEAD, (hd + 1) * HG_HEAD)
                v = p_ref[prow, VC + hd * HG_HEAD:VC + (hd + 1) * HG_HEAD].astype(BF16)
                o = _dot(a_ref[hd, prow, 0:CHUNK], v) + _dot_nt(qi_ref[prow, hs], sprev_ref[c * HG_HEADS + hd])
                o = o * lax.rsqrt(jnp.mean(o * o, axis=-1, keepdims=True) + EPS) * gn
                gg = p_ref[prow, GC + hd * HG_HEAD:GC + (hd + 1) * HG_HEAD]
                y_ref[prow, D_POOL + hd * HG_HEAD:D_POOL + (hd + 1) * HG_HEAD] = (o * _silu(gg)).astype(BF16)

        hist_ref[...] = p_ref[TL_MIX - HIST:TL_MIX, 0:D_POOL]
        out_ref[...] = h_ref[...] + _dot(y_ref[...], wout_ref[...])

    _mixer_steps(h_ref, hn_ref, g_ref, win_ref, pa_ref, pb_ref, reset, mix)


def _even_call(h, g, win, poolw, pscale, lbl, gn, wout, tril):
    consts = (g, win, poolw, pscale, lbl, gn, wout, tril)
    scratch = [pltpu.VMEM((HG_HEADS, HG_HEAD, HG_HEAD), F32),
               pltpu.VMEM((HIST, D_POOL), F32),
               pltpu.VMEM((TL_MIX, D_HGRN), BF16),
               pltpu.VMEM((TL_MIX, D_HGRN), BF16),
               pltpu.VMEM((TL_MIX, D_HGRN), BF16),
               pltpu.VMEM((TL_MIX, D_HGRN), BF16),
               pltpu.VMEM((SUBLANES, D_HGRN), F32),
               pltpu.VMEM((HG_HEADS, TL_MIX, LANES), BF16),
               pltpu.VMEM((N_CHUNKS * HG_HEADS, HG_HEAD, HG_HEAD), BF16),
               pltpu.VMEM((CHUNK, D_HGRN), F32),
               pltpu.VMEM((CHUNK, D_HGRN), F32)]
    return _mixer_call(_even_kernel, "even_mixer", h, consts, D_IN_EVEN, scratch)


def _odd_kernel(h_ref, hn_ref, g_ref, win_ref, cw_ref, cb_ref, lng_ref, lnb_ref, lw_ref, lcb_ref, wax_ref,
                bax_ref, lam_ref, wout_ref, out_ref, pa_ref, pb_ref, y_ref, hc_ref, hu_ref, hx_ref):
    XB = 2 * D_CONV
    GC = 2 * D_CONV + D_LRU

    def reset():
        hc_ref[...] = jnp.zeros_like(hc_ref)
        hu_ref[...] = jnp.zeros_like(hu_ref)
        hx_ref[...] = jnp.zeros_like(hx_ref)

    def mix(p_ref, project_piece):
        lam = lam_ref[...]
        neg_c_softplus = -LRU_C * (jnp.maximum(-lam, 0.0) + jnp.log1p(jnp.exp(-jnp.abs(lam))))
        bax = bax_ref[...]
        sub = lax.broadcasted_iota(jnp.int32, (CHUNK, D_LRU), 0) & (SUBLANES - 1)
        row0 = lax.broadcasted_iota(jnp.int32, (CHUNK, D_LRU), 0) + _tile_row0()
        carry = hc_ref[0:1, :]

        for c in range(N_CHUNKS):
            project_piece(c)
            r0 = c * CHUNK
            prow = slice(r0, r0 + CHUNK)

            p_ref[prow, 0:D_CONV] = p_ref[prow, 0:D_CONV] * _sigmoid(p_ref[prow, D_CONV:2 * D_CONV])
            for gi in range(CONV_GROUPS):
                gs = slice(gi * LANES, (gi + 1) * LANES)
                if c == 0:
                    uw = jnp.concatenate([hu_ref[:, gs], p_ref[0:CHUNK, gs]], axis=0)
                else:
                    uw = p_ref[r0 - HIST:r0 + CHUNK, gs]
                acc = None
                for r in range(SUBLANES):
                    part = None
                    for a in range((CONV_WIDTH - 1 - r) // SUBLANES + 1):
                        j = CONV_WIDTH - 1 - (SUBLANES * a + r)
                        lo = HIST - SUBLANES * (a + 1)
                        term = cw_ref[j:j + 1, gs] * uw[lo:lo + CHUNK + SUBLANES]
                        part = term if part is None else part + term
                    if r:
                        part = pltpu.roll(part, r, axis=0)
                    acc = part if acc is None else acc + part
                ug = acc[SUBLANES:] + cb_ref[:, gs]
                mu = jnp.mean(ug, axis=-1, keepdims=True)
                dv = ug - mu
                var = jnp.mean(dv * dv, axis=-1, keepdims=True)
                yn = dv * lax.rsqrt(var + EPS) * lng_ref[:, gs] + lnb_ref[:, gs]
                y_ref[prow, gs] = _silu(yn).astype(BF16)

            if c == 0:
                xw = jnp.concatenate([hx_ref[HIST - SUBLANES:HIST, :], p_ref[0:CHUNK, XB:XB + D_LRU]], axis=0)
            else:
                xw = p_ref[r0 - SUBLANES:r0 + CHUNK, XB:XB + D_LRU]
            u = lw_ref[LRU_CONV - 1:LRU_CONV, :] * xw
            for d in range(1, LRU_CONV):
                u = u + lw_ref[LRU_CONV - 1 - d:LRU_CONV - d, :] * pltpu.roll(xw, d, axis=0)
            u = u[SUBLANES:] + lcb_ref[...]
            gates = _dot(u.astype(BF16), wax_ref[...]) + bax
            r = _sigmoid(gates[:, 0:D_LRU])
            ig = _sigmoid(gates[:, D_LRU:2 * D_LRU])
            log_a = r * neg_c_softplus
            a = jnp.exp(log_a)
            mult = jnp.sqrt(jnp.tanh(-log_a) * (a * a + 1.0))
            first = (row0 + r0) == N_PAD
            bt = jnp.where(first, 1.0, mult) * (ig * u)
            a = jnp.where(first, 0.0, a)
            for sh in (1, 2, 4):
                keep = sub >= sh
                a_sh = jnp.where(keep, pltpu.roll(a, sh, axis=0), 1.0)
                b_sh = jnp.where(keep, pltpu.roll(bt, sh, axis=0), 0.0)
                bt = bt + a * b_sh
                a = a * a_sh
            hs = []
            for blk in range(CHUNK // SUBLANES):
                rs = slice(blk * SUBLANES, (blk + 1) * SUBLANES)
                hb = bt[rs, :] + a[rs, :] * carry
                carry = hb[SUBLANES - 1:SUBLANES, :]
                hs.append(hb)
            hseq = jnp.concatenate(hs, axis=0)
            dg = p_ref[prow, GC:GC + D_LRU]
            gelu = 0.5 * dg * (1.0 + jnp.tanh(0.7978845608028654 * (dg + 0.044715 * dg * dg * dg)))
            y_ref[prow, D_CONV:D_CONV + D_LRU] = (gelu * hseq).astype(BF16)

        hc_ref[0:1, :] = carry
        hu_ref[...] = p_ref[TL_MIX - HIST:TL_MIX, 0:D_CONV]
        hx_ref[...] = p_ref[TL_MIX - HIST:TL_MIX, XB:XB + D_LRU]
        out_ref[...] = h_ref[...] + _dot(y_ref[...], wout_ref[...])

    _mixer_steps(h_ref, hn_ref, g_ref, win_ref, pa_ref, pb_ref, reset, mix)


def _odd_call(h, g, win, cw, cb, lng, lnb, lw, lcb, wax, bax, lam, wout):
    consts = (g, win, cw, cb, lng, lnb, lw, lcb, wax, bax, lam, wout)
    scratch = [pltpu.VMEM((SUBLANES, D_LRU), F32),
               pltpu.VMEM((HIST, D_CONV), F32),
               pltpu.VMEM((HIST, D_LRU), F32)]
    return _mixer_call(_odd_kernel, "odd_mixer", h, consts, D_IN_ODD, scratch)


def _block_diag(w):
    g, n, _ = w.shape
    eye = jnp.eye(g, dtype=w.dtype)
    return jnp.einsum('gij,gh->gihj', w, eye).reshape(g * n, g * n)


def kernel(x, meta_tokens, ffn1_norm, ffn1_wg, ffn1_wu, ffn1_wd, mix_norm, ffn2_norm, ffn2_wg, ffn2_wu, ffn2_wd, w_in_even, pool_w, pool_scale, hgrn_lb_logits, hgrn_gnorm, w_out_even, w_in_odd, conv_w, conv_b, conv_ln_g, conv_ln_b, lru_conv_w, lru_conv_b, lru_wa, lru_ba, lru_wx, lru_bx, lru_lambda, w_out_odd, final_norm):
    bn = x.shape[0]
    x_flat = x.astype(F32).reshape(bn * SEQ, D_MODEL)
    head = jnp.concatenate([jnp.zeros((N_PAD, D_MODEL), F32), meta_tokens.astype(F32)], axis=0)

    row = lambda a: a.reshape(1, -1).astype(F32)
    bf = lambda a: a.astype(BF16)

    f32 = lambda a: a.astype(F32)
    tril = jnp.tril(jnp.ones((CHUNK, CHUNK), BF16))

    h = _ffn_first_call(x_flat, head, 0, row(ffn1_norm[0]), f32(ffn1_wg), f32(ffn1_wu), f32(ffn1_wd))
    h = _even_call(h, row(mix_norm[0]), bf(w_in_even[0]), bf(_block_diag(pool_w[0])), row(pool_scale[0]),
                   hgrn_lb_logits.astype(F32), row(hgrn_gnorm[0]), bf(w_out_even[0]), tril)
    h = _ffn_mid_call(h, 0, row(ffn2_norm[0]), f32(ffn2_wg), f32(ffn2_wu), f32(ffn2_wd))
    h = _ffn_mid_call(h, 1, row(ffn1_norm[1]), f32(ffn1_wg), f32(ffn1_wu), f32(ffn1_wd))
    wax = jnp.concatenate([_block_diag(lru_wa[0]), _block_diag(lru_wx[0])], axis=1)
    bax = jnp.concatenate([lru_ba[0], lru_bx[0]]).reshape(1, -1).astype(F32)
    h = _odd_call(h, row(mix_norm[1]), bf(w_in_odd[0]), conv_w[0].astype(F32), row(conv_b[0]),
                  row(conv_ln_g[0]), row(conv_ln_b[0]), lru_conv_w[0].astype(F32), row(lru_conv_b[0]),
                  bf(wax), bax, row(lru_lambda[0]), bf(w_out_odd[0]))
    out = _ffn_final_call(h, 1, row(ffn2_norm[1]), f32(ffn2_wg), f32(ffn2_wu), f32(ffn2_wd), row(final_norm))
    return out.reshape(bn, SEQ, D_MODEL)
```

```python
import functools

import jax
import jax.numpy as jnp
from jax import lax
from jax.experimental import pallas as pl
from jax.experimental.pallas import tpu as pltpu

F32 = jnp.float32
BF16 = jnp.bfloat16

D_MODEL = 1024
BATCH = 8
SEQ = 4096
N_META = 16
D_FF = 2816
EPS = 1e-6
D_POOL = 256
HG_HEAD = 128
D_HGRN = 768
HG_HEADS = 6
CHUNK = 64
D_IN_EVEN = D_POOL + 4 * D_HGRN
D_CONV = 512
CONV_WIDTH = 31
CONV_GROUPS = 4
D_LRU = 512
LRU_CONV = 4
LRU_C = 8.0
D_IN_ODD = 2 * D_CONV + 2 * D_LRU
LANES = 128
SUBLANES = 8

HEAD = CHUNK
N_PAD = HEAD - N_META
L_PAD = HEAD + SEQ
ROWS = BATCH * L_PAD

TM_FFN = 640
TM_OUT = 512
FFN_SUBTILES = 2
FF_BOUNDS = (0, 1536, D_FF)
TL_MIX = 320
N_CHUNKS = TL_MIX // CHUNK
TILES_PER_SEQ = L_PAD // TL_MIX
N_TILES = ROWS // TL_MIX
HIST = 32
VMEM_LIMIT = 58 * 1024 * 1024


def _rms(x, g):
    ms = jnp.mean(x * x, axis=-1, keepdims=True)
    return x * lax.rsqrt(ms + EPS) * g


NEG_LOG2E = -1.4426950408889634


def _exp_neg(x):
    return jnp.exp2(x * NEG_LOG2E)


def _sigmoid(x):
    return 1.0 / (1.0 + _exp_neg(x))


def _silu(x):
    return x * _sigmoid(x)


def _dot(a, b):
    return jnp.dot(a, b, preferred_element_type=F32)


def _dot_nt(a, b):
    return lax.dot_general(a, b, (((1,), (1,)), ((), ())), preferred_element_type=F32)


def _dot_tn(a, b):
    return lax.dot_general(a, b, (((0,), (0,)), ((), ())), preferred_element_type=F32)


def _resident():
    return pl.BlockSpec(memory_space=pltpu.VMEM)


N_WSTEPS = 8
WG_SLAB = D_MODEL // N_WSTEPS
WD_SLAB = D_FF // N_WSTEPS


def _swiglu(xn, wg_ref, wu_ref, wd_ref):
    acc = None
    for lo, hi in zip(FF_BOUNDS[:-1], FF_BOUNDS[1:]):
        gate = _dot(xn, wg_ref[:, lo:hi])
        up = _dot(xn, wu_ref[:, lo:hi])
        hid = (_silu(gate) * up).astype(BF16)
        part = _dot(hid, wd_ref[lo:hi, :])
        acc = part if acc is None else acc + part
    return acc


def _ffn_kernel(splice_head, final, n_sub, *refs):
    wg_ref, wu_ref, wd_ref = refs[-3:]
    out_ref = refs[-4]
    nxt = n_sub + 1 if splice_head else 1
    g_ref, wg_slab, wu_slab, wd_slab = refs[nxt:nxt + 4]
    step = pl.program_id(0)

    @pl.when(step < N_WSTEPS)
    def _():
        rg = pl.ds(pl.multiple_of(step * WG_SLAB, WG_SLAB), WG_SLAB)
        wg_ref[rg, :] = wg_slab[...].astype(BF16)
        wu_ref[rg, :] = wu_slab[...].astype(BF16)
        wd_ref[pl.ds(pl.multiple_of(step * WD_SLAB, 16), WD_SLAB), :] = wd_slab[...].astype(BF16)

    @pl.when(step >= N_WSTEPS)
    def _():
        sub = out_ref.shape[0] // n_sub
        for r in range(n_sub):
            rows = slice(r * sub, (r + 1) * sub)
            if splice_head:
                h = refs[r][...]
                first = ((step - N_WSTEPS) * n_sub + r) % TILES_PER_SEQ == 0
                spliced = jnp.concatenate([refs[n_sub][...], h[0:TL_MIX - HEAD]], axis=0)
                h = jnp.where(first, spliced, h)
            else:
                h = refs[0][rows, :]
            xn = _rms(h, g_ref[...]).astype(BF16)
            h = h + 0.5 * _swiglu(xn, wg_ref, wu_ref, wd_ref)
            if final:
                h = _rms(h, refs[nxt + 4][...])
            out_ref[rows, :] = h


def _tile_of(step):
    return jnp.maximum(step - N_WSTEPS, 0)


def _ffn_pallas(name, kernel_fn, n_tiles, row_specs, out_spec, out_rows, row_args, layer, g, wg, wu, wd, extra):
    slab = lambda s: jnp.minimum(s, N_WSTEPS - 1)
    w_specs = [pl.BlockSpec((None, WG_SLAB, D_FF), lambda s: (layer, slab(s), 0)),
               pl.BlockSpec((None, WG_SLAB, D_FF), lambda s: (layer, slab(s), 0)),
               pl.BlockSpec((None, WD_SLAB, D_MODEL), lambda s: (layer, slab(s), 0))]
    n_res = len(row_args) - len(row_specs)
    return pl.pallas_call(
        kernel_fn,
        out_shape=jax.ShapeDtypeStruct((out_rows, D_MODEL), F32),
        grid=(N_WSTEPS + n_tiles,),
        in_specs=list(row_specs) + [_resident()] * (n_res + 1) + w_specs + [_resident()] * len(extra),
        out_specs=out_spec,
        scratch_shapes=[pltpu.VMEM((D_MODEL, D_FF), BF16), pltpu.VMEM((D_MODEL, D_FF), BF16),
                        pltpu.VMEM((D_FF, D_MODEL), BF16)],
        compiler_params=pltpu.CompilerParams(
            dimension_semantics=("arbitrary",), vmem_limit_bytes=VMEM_LIMIT),
        name=name,
    )(*row_args, g, wg, wu, wd, *extra)


def _x_tile_offset(n):
    return CHUNK * ((n // TILES_PER_SEQ) * (SEQ // CHUNK) + jnp.maximum((n % TILES_PER_SEQ) * N_CHUNKS - 1, 0))


def _ffn_first_call(x_flat, head, layer, g, wg, wu, wd):
    n_sub = FFN_SUBTILES
    in_specs = [pl.BlockSpec((pl.Element(TL_MIX), pl.Element(D_MODEL)),
                             lambda s, r=r: (_x_tile_offset(_tile_of(s) * n_sub + r), 0)) for r in range(n_sub)]
    out_spec = pl.BlockSpec((n_sub * TL_MIX, D_MODEL), lambda s: (_tile_of(s), 0))
    return _ffn_pallas("ffn_first", functools.partial(_ffn_kernel, True, False, n_sub), N_TILES // n_sub, in_specs,
                       out_spec, ROWS, (x_flat,) * n_sub + (head,), layer, g, wg, wu, wd, ())


def _ffn_mid_call(h, layer, g, wg, wu, wd):
    spec = pl.BlockSpec((TM_FFN, D_MODEL), lambda s: (_tile_of(s), 0))
    return _ffn_pallas("ffn_mid", functools.partial(_ffn_kernel, False, False, FFN_SUBTILES), ROWS // TM_FFN,
                       [spec], spec, ROWS, (h,), layer, g, wg, wu, wd, ())


def _ffn_final_call(h, layer, g, wg, wu, wd, final_g):
    per_seq = SEQ // TM_OUT

    def in_row(s):
        i = _tile_of(s)
        return CHUNK * ((i // per_seq) * (L_PAD // CHUNK) + HEAD // CHUNK + (i % per_seq) * (TM_OUT // CHUNK))

    in_spec = pl.BlockSpec((pl.Element(TM_OUT), pl.Element(D_MODEL)), lambda s: (in_row(s), 0))
    out_spec = pl.BlockSpec((TM_OUT, D_MODEL), lambda s: (_tile_of(s), 0))
    return _ffn_pallas("ffn_final", functools.partial(_ffn_kernel, False, True, FFN_SUBTILES), BATCH * per_seq,
                       [in_spec], out_spec, BATCH * SEQ, (h,), layer, g, wg, wu, wd, (final_g,))


def _mixer_steps(h_ref, hn_ref, g_ref, win_ref, pa_ref, pb_ref, reset, mix):
    j = pl.program_id(0)
    g = g_ref[...]
    d_in = win_ref.shape[1]
    n_col_tiles = d_in // (2 * LANES)
    bounds = [2 * LANES * ((n_col_tiles * i) // N_CHUNKS) for i in range(N_CHUNKS + 1)]

    @pl.when(j == 0)
    def _():
        pa_ref[...] = _dot(_rms(h_ref[0:TL_MIX, :], g).astype(BF16), win_ref[...])

    for half, (cur_ref, nxt_ref) in enumerate(((pa_ref, pb_ref), (pb_ref, pa_ref))):
        tile = 2 * j + half

        @pl.when(tile % TILES_PER_SEQ == 0)
        def _():
            reset()

        nxt_rows = h_ref[TL_MIX:2 * TL_MIX, :] if half == 0 else hn_ref[...]
        un = _rms(nxt_rows, g).astype(BF16)

        def project_piece(c, un=un, nxt_ref=nxt_ref):
            lo, hi = bounds[c], bounds[c + 1]
            if hi > lo:
                nxt_ref[:, lo:hi] = _dot(un, win_ref[:, lo:hi])

        mix(cur_ref, project_piece, (tile % TILES_PER_SEQ) * TL_MIX, slice(half * TL_MIX, (half + 1) * TL_MIX))


def _mixer_call(body, name, h, consts, d_in, scratch):
    pair_spec = pl.BlockSpec((2 * TL_MIX, D_MODEL), lambda j: (j, 0))
    nxt_spec = pl.BlockSpec((TL_MIX, D_MODEL), lambda j: (jnp.minimum(2 * j + 2, N_TILES - 1), 0))
    return pl.pallas_call(
        body,
        out_shape=jax.ShapeDtypeStruct((ROWS, D_MODEL), F32),
        grid=(N_TILES // 2,),
        in_specs=[pair_spec, nxt_spec] + [_resident()] * len(consts),
        out_specs=pair_spec,
        scratch_shapes=[pltpu.VMEM((TL_MIX, d_in), F32),
                        pltpu.VMEM((TL_MIX, d_in), F32),
                        pltpu.VMEM((TL_MIX, D_MODEL), BF16),
                        ] + scratch,
        compiler_params=pltpu.CompilerParams(
            dimension_semantics=("arbitrary",), vmem_limit_bytes=VMEM_LIMIT),
        name=name,
    )(h, h, *consts)


DECAY_SAFE = 150.0


def _even_kernel(h_ref, hn_ref, g_ref, win_ref, poolw_ref, pscale_ref, lbl_ref, gn_ref, wout_ref, tril_ref,
                 out_ref, pa_ref, pb_ref, y_ref, st_ref, hist_ref, qt_ref, kt_ref, qi_ref, ks_ref, dec_ref, a_ref,
                 sprev_ref, bs_ref, kk_ref):
    QC = D_POOL
    FC = D_POOL + D_HGRN
    VC = D_POOL + 2 * D_HGRN
    GC = D_POOL + 3 * D_HGRN

    def reset():
        st_ref[...] = jnp.zeros_like(st_ref)
        hist_ref[...] = jnp.zeros_like(hist_ref)

    def mix(p_ref, project_piece, tile_row0, hrows):
        lbl = lbl_ref[...]
        mx = jnp.maximum(lbl[0:1, :], lbl[1:2, :])
        e0 = jnp.exp(lbl[0:1, :] - mx)
        e1 = jnp.exp(lbl[1:2, :] - mx)
        lb = e0 / (e0 + e1)
        log_lb = jnp.log(lb)
        log_1mlb = jnp.log1p(-lb)
        one_mlb = 1.0 - lb
        gn = gn_ref[...]
        tril = tril_ref[...]
        causal = tril > 0.5
        pscale = pscale_ref[...]
        lane = lax.broadcasted_iota(jnp.int32, (CHUNK, D_POOL), 1)
        win = jnp.where(lane < 64, 2.0, jnp.where(lane < 128, 4.0, jnp.where(lane < 192, 8.0, 16.0)))
        pos = lax.broadcasted_iota(jnp.int32, (CHUNK, D_POOL), 0) + (tile_row0 + 1 - N_PAD)

        def gates(prow):
            q = _silu(p_ref[prow, QC:QC + D_HGRN])
            z = p_ref[prow, FC:FC + D_HGRN]
            ez = _exp_neg(jnp.abs(z))
            wz = 1.0 + ez
            log_sig = jnp.minimum(z, 0.0) - jnp.log(wz)
            rz = 1.0 / wz
            k = one_mlb * jnp.where(z >= 0, ez * rz, rz)
            bb = log_1mlb + log_sig
            log_f = jnp.maximum(log_lb, bb) + jnp.log(1.0 + _exp_neg(jnp.abs(log_lb - bb)))
            lf_hi = log_f.astype(BF16)
            lf_lo = (log_f - lf_hi.astype(F32)).astype(BF16)
            b = _dot(tril, lf_hi) + _dot(tril, lf_lo)
            return q, k, b

        worst = jnp.zeros((1, D_HGRN), F32)
        for c in range(N_CHUNKS):
            project_piece(c)
            r0 = c * CHUNK
            prow = slice(r0, r0 + CHUNK)

            if c == 0:
                w = jnp.concatenate([hist_ref[HIST - 16:HIST, :], p_ref[0:CHUNK, 0:D_POOL]], axis=0)
            else:
                w = p_ref[r0 - 16:r0 + CHUNK, 0:D_POOL]
            s2 = w + pltpu.roll(w, 1, axis=0)
            s4 = s2 + pltpu.roll(s2, 2, axis=0)
            s8 = s4 + pltpu.roll(s4, 4, axis=0)
            s16 = s8 + pltpu.roll(s8, 8, axis=0)
            x = w[16:]
            ssel = jnp.where(lane < 64, s2[16:],
                             jnp.where(lane < 128, s4[16:], jnp.where(lane < 192, s8[16:], s16[16:])))
            cnt = jnp.minimum(jnp.maximum(pos + r0, 1).astype(F32), win)
            mixed = ssel / cnt - x
            ya = _dot(mixed.astype(BF16), poolw_ref[...]) * pscale
            y_ref[prow, 0:D_POOL] = ya.astype(BF16)

            q, k, b = gates(prow)
            b_end = b[CHUNK - 1:CHUNK, :]
            bm = b - 0.5 * b_end
            qt_ref[prow, :] = (q * jnp.exp(bm)).astype(BF16)
            kt_ref[prow, :] = (k * _exp_neg(bm)).astype(BF16)
            qi_ref[prow, :] = (q * jnp.exp(b)).astype(BF16)
            ks_ref[prow, :] = (k * jnp.exp(b_end - b)).astype(BF16)
            dec_ref[c:c + 1, :] = jnp.exp(b_end)
            worst = jnp.maximum(worst, -b_end)

        for c in range(N_CHUNKS):
            prow = slice(c * CHUNK, (c + 1) * CHUNK)
            for hd in range(HG_HEADS):
                hs = slice(hd * HG_HEAD, (hd + 1) * HG_HEAD)
                a = jnp.where(causal, _dot_nt(qt_ref[prow, hs], kt_ref[prow, hs]), 0.0)
                a_ref[hd, prow, 0:CHUNK] = a.astype(BF16)
                v = p_ref[prow, VC + hd * HG_HEAD:VC + (hd + 1) * HG_HEAD].astype(BF16)
                st = st_ref[hd]
                sprev_ref[c * HG_HEADS + hd] = st.astype(BF16)
                st_ref[hd] = st * dec_ref[c:c + 1, hs] + _dot_tn(v, ks_ref[prow, hs])

        @pl.when(jnp.max(worst) > DECAY_SAFE)
        def _():
            col = lax.broadcasted_iota(jnp.int32, (CHUNK, CHUNK), 1)
            for c in range(N_CHUNKS):
                prow = slice(c * CHUNK, (c + 1) * CHUNK)
                q, k, b = gates(prow)
                bs_ref[...] = b
                kk_ref[...] = k

                def column(s, accs):
                    wgt = q * jnp.exp(jnp.minimum(b - bs_ref[pl.ds(s, 1), :], 0.0)) * kk_ref[pl.ds(s, 1), :]
                    out = []
                    for hd in range(HG_HEADS):
                        d = jnp.sum(wgt[:, hd * HG_HEAD:(hd + 1) * HG_HEAD], axis=-1, keepdims=True)
                        out.append(jnp.where(col == s, d, accs[hd]))
                    return tuple(out)

                accs = lax.fori_loop(0, CHUNK, column, tuple(jnp.zeros((CHUNK, CHUNK), F32) for _ in range(HG_HEADS)))
                for hd in range(HG_HEADS):
                    a_ref[hd, prow, 0:CHUNK] = jnp.where(causal, accs[hd], 0.0).astype(BF16)

        for c in range(N_CHUNKS):
            prow = slice(c * CHUNK, (c + 1) * CHUNK)
            for hd in range(HG_HEADS):
                hs = slice(hd * HG_HEAD, (hd + 1) * HG_HEAD)
                v = p_ref[prow, VC + hd * HG_HEAD:VC + (hd + 1) * HG_HEAD].astype(BF16)
                o = _dot(a_ref[hd, prow, 0:CHUNK], v) + _dot_nt(qi_ref[prow, hs], sprev_ref[c * HG_HEADS + hd])
                o = o * lax.rsqrt(jnp.mean(o * o, axis=-1, keepdims=True) + EPS) * gn
                gg = p_ref[prow, GC + hd * HG_HEAD:GC + (hd + 1) * HG_HEAD]
                y_ref[prow, D_POOL + hd * HG_HEAD:D_POOL + (hd + 1) * HG_HEAD] = (o * _silu(gg)).astype(BF16)

        hist_ref[...] = p_ref[TL_MIX - HIST:TL_MIX, 0:D_POOL]
        out_ref[hrows, :] = h_ref[hrows, :] + _dot(y_ref[...], wout_ref[...])

    _mixer_steps(h_ref, hn_ref, g_ref, win_ref, pa_ref, pb_ref, reset, mix)


def _even_call(h, g, win, poolw, pscale, lbl, gn, wout, tril):
    consts = (g, win, poolw, pscale, lbl, gn, wout, tril)
    scratch = [pltpu.VMEM((HG_HEADS, HG_HEAD, HG_HEAD), F32),
               pltpu.VMEM((HIST, D_POOL), F32),
               pltpu.VMEM((TL_MIX, D_HGRN), BF16),
               pltpu.VMEM((TL_MIX, D_HGRN), BF16),
               pltpu.VMEM((TL_MIX, D_HGRN), BF16),
               pltpu.VMEM((TL_MIX, D_HGRN), BF16),
               pltpu.VMEM((SUBLANES, D_HGRN), F32),
               pltpu.VMEM((HG_HEADS, TL_MIX, LANES), BF16),
               pltpu.VMEM((N_CHUNKS * HG_HEADS, HG_HEAD, HG_HEAD), BF16),
               pltpu.VMEM((CHUNK, D_HGRN), F32),
               pltpu.VMEM((CHUNK, D_HGRN), F32)]
    return _mixer_call(_even_kernel, "even_mixer", h, consts, D_IN_EVEN, scratch)


def _odd_kernel(h_ref, hn_ref, g_ref, win_ref, cw_ref, cb_ref, lng_ref, lnb_ref, lw_ref, lcb_ref, wax_ref,
                bax_ref, lam_ref, wout_ref, out_ref, pa_ref, pb_ref, y_ref, hc_ref, hu_ref, hx_ref):
    XB = 2 * D_CONV
    GC = 2 * D_CONV + D_LRU

    def reset():
        hc_ref[...] = jnp.zeros_like(hc_ref)
        hu_ref[...] = jnp.zeros_like(hu_ref)
        hx_ref[...] = jnp.zeros_like(hx_ref)

    def mix(p_ref, project_piece, tile_row0, hrows):
        lam = lam_ref[...]
        neg_c_softplus = -LRU_C * (jnp.maximum(-lam, 0.0) + jnp.log1p(jnp.exp(-jnp.abs(lam))))
        bax = bax_ref[...]
        sub = lax.broadcasted_iota(jnp.int32, (CHUNK, D_LRU), 0) & (SUBLANES - 1)
        row0 = lax.broadcasted_iota(jnp.int32, (CHUNK, D_LRU), 0) + tile_row0
        carry = hc_ref[0:1, :]

        for c in range(N_CHUNKS):
            project_piece(c)
            r0 = c * CHUNK
            prow = slice(r0, r0 + CHUNK)

            p_ref[prow, 0:D_CONV] = p_ref[prow, 0:D_CONV] * _sigmoid(p_ref[prow, D_CONV:2 * D_CONV])
            for gi in range(CONV_GROUPS):
                gs = slice(gi * LANES, (gi + 1) * LANES)
                if c == 0:
                    uw = jnp.concatenate([hu_ref[:, gs], p_ref[0:CHUNK, gs]], axis=0)
                else:
                    uw = p_ref[r0 - HIST:r0 + CHUNK, gs]
                acc = None
                for r in range(SUBLANES):
                    part = None
                    for a in range((CONV_WIDTH - 1 - r) // SUBLANES + 1):
                        j = CONV_WIDTH - 1 - (SUBLANES * a + r)
                        lo = HIST - SUBLANES * (a + 1)
                        term = cw_ref[j:j + 1, gs] * uw[lo:lo + CHUNK + SUBLANES]
                        part = term if part is None else part + term
                    if r:
                        part = pltpu.roll(part, r, axis=0)
                    acc = part if acc is None else acc + part
                ug = acc[SUBLANES:] + cb_ref[:, gs]
                mu = jnp.mean(ug, axis=-1, keepdims=True)
                dv = ug - mu
                var = jnp.mean(dv * dv, axis=-1, keepdims=True)
                yn = dv * lax.rsqrt(var + EPS) * lng_ref[:, gs] + lnb_ref[:, gs]
                y_ref[prow, gs] = _silu(yn).astype(BF16)

            if c == 0:
                xw = jnp.concatenate([hx_ref[HIST - SUBLANES:HIST, :], p_ref[0:CHUNK, XB:XB + D_LRU]], axis=0)
            else:
                xw = p_ref[r0 - SUBLANES:r0 + CHUNK, XB:XB + D_LRU]
            u = lw_ref[LRU_CONV - 1:LRU_CONV, :] * xw
            for d in range(1, LRU_CONV):
                u = u + lw_ref[LRU_CONV - 1 - d:LRU_CONV - d, :] * pltpu.roll(xw, d, axis=0)
            u = u[SUBLANES:] + lcb_ref[...]
            gates = _dot(u.astype(BF16), wax_ref[...]) + bax
            r = _sigmoid(gates[:, 0:D_LRU])
            ig = _sigmoid(gates[:, D_LRU:2 * D_LRU])
            log_a = r * neg_c_softplus
            a = jnp.exp(log_a)
            mult = jnp.sqrt(jnp.tanh(-log_a) * (a * a + 1.0))
            first = (row0 + r0) == N_PAD
            bt = jnp.where(first, 1.0, mult) * (ig * u)
            a = jnp.where(first, 0.0, a)
            for sh in (1, 2, 4):
                keep = sub >= sh
                a_sh = jnp.where(keep, pltpu.roll(a, sh, axis=0), 1.0)
                b_sh = jnp.where(keep, pltpu.roll(bt, sh, axis=0), 0.0)
                bt = bt + a * b_sh
                a = a * a_sh
            hs = []
            for blk in range(CHUNK // SUBLANES):
                rs = slice(blk * SUBLANES, (blk + 1) * SUBLANES)
                hb = bt[rs, :] + a[rs, :] * carry
                carry = hb[SUBLANES - 1:SUBLANES, :]
                hs.append(hb)
            hseq = jnp.concatenate(hs, axis=0)
            dg = p_ref[prow, GC:GC + D_LRU]
            gelu = 0.5 * dg * (1.0 + jnp.tanh(0.7978845608028654 * (dg + 0.044715 * dg * dg * dg)))
            y_ref[prow, D_CONV:D_CONV + D_LRU] = (gelu * hseq).astype(BF16)

        hc_ref[0:1, :] = carry
        hu_ref[...] = p_ref[TL_MIX - HIST:TL_MIX, 0:D_CONV]
        hx_ref[...] = p_ref[TL_MIX - HIST:TL_MIX, XB:XB + D_LRU]
        out_ref[hrows, :] = h_ref[hrows, :] + _dot(y_ref[...], wout_ref[...])

    _mixer_steps(h_ref, hn_ref, g_ref, win_ref, pa_ref, pb_ref, reset, mix)


def _odd_call(h, g, win, cw, cb, lng, lnb, lw, lcb, wax, bax, lam, wout):
    consts = (g, win, cw, cb, lng, lnb, lw, lcb, wax, bax, lam, wout)
    scratch = [pltpu.VMEM((SUBLANES, D_LRU), F32),
               pltpu.VMEM((HIST, D_CONV), F32),
               pltpu.VMEM((HIST, D_LRU), F32)]
    return _mixer_call(_odd_kernel, "odd_mixer", h, consts, D_IN_ODD, scratch)


def _block_diag(w):
    g, n, _ = w.shape
    eye = jnp.eye(g, dtype=w.dtype)
    return jnp.einsum('gij,gh->gihj', w, eye).reshape(g * n, g * n)


def kernel(x, meta_tokens, ffn1_norm, ffn1_wg, ffn1_wu, ffn1_wd, mix_norm, ffn2_norm, ffn2_wg, ffn2_wu, ffn2_wd, w_in_even, pool_w, pool_scale, hgrn_lb_logits, hgrn_gnorm, w_out_even, w_in_odd, conv_w, conv_b, conv_ln_g, conv_ln_b, lru_conv_w, lru_conv_b, lru_wa, lru_ba, lru_wx, lru_bx, lru_lambda, w_out_odd, final_norm):
    bn = x.shape[0]
    x_flat = x.astype(F32).reshape(bn * SEQ, D_MODEL)
    head = jnp.concatenate([jnp.zeros((N_PAD, D_MODEL), F32), meta_tokens.astype(F32)], axis=0)

    row = lambda a: a.reshape(1, -1).astype(F32)
    bf = lambda a: a.astype(BF16)

    f32 = lambda a: a.astype(F32)
    tril = jnp.tril(jnp.ones((CHUNK, CHUNK), BF16))

    h = _ffn_first_call(x_flat, head, 0, row(ffn1_norm[0]), f32(ffn1_wg), f32(ffn1_wu), f32(ffn1_wd))
    h = _even_call(h, row(mix_norm[0]), bf(w_in_even[0]), bf(_block_diag(pool_w[0])), row(pool_scale[0]),
                   hgrn_lb_logits.astype(F32), row(hgrn_gnorm[0]), bf(w_out_even[0]), tril)
    h = _ffn_mid_call(h, 0, row(ffn2_norm[0]), f32(ffn2_wg), f32(ffn2_wu), f32(ffn2_wd))
    h = _ffn_mid_call(h, 1, row(ffn1_norm[1]), f32(ffn1_wg), f32(ffn1_wu), f32(ffn1_wd))
    wax = jnp.concatenate([_block_diag(lru_wa[0]), _block_diag(lru_wx[0])], axis=1)
    bax = jnp.concatenate([lru_ba[0], lru_bx[0]]).reshape(1, -1).astype(F32)
    h = _odd_call(h, row(mix_norm[1]), bf(w_in_odd[0]), conv_w[0].astype(F32), row(conv_b[0]),
                  row(conv_ln_g[0]), row(conv_ln_b[0]), lru_conv_w[0].astype(F32), row(lru_conv_b[0]),
                  bf(wax), bax, row(lru_lambda[0]), bf(w_out_odd[0]))
    out = _ffn_final_call(h, 1, row(ffn2_norm[1]), f32(ffn2_wg), f32(ffn2_wu), f32(ffn2_wd), row(final_norm))
    return out.reshape(bn, SEQ, D_MODEL)
```
